```python
import math
import jax, jax.numpy as jnp
from jax import lax
import numpy as np

D_MODEL = 4096
BATCH = 1
SEQ = 8192
DEPTH = 4

CHUNK = 64
QBLOCK = 128
N_MIXERS = 3
SB_HEADS = 32
SB_HEAD_DIM = D_MODEL // SB_HEADS
ML_HEADS = 8
ML_QK_DIM = D_MODEL // ML_HEADS // 2
ML_V_DIM = D_MODEL // ML_HEADS
CONV_WIDTH = 3
D_FF = 4 * D_MODEL
DEEPNORM_ALPHA = (2 * DEPTH) ** 0.25
DEEPNORM_BETA = (8 * DEPTH) ** -0.25
LN_EPS = 1e-5
HEAD_NORM_EPS = 1e-6
N_SB_LAYERS = (DEPTH + 2) // 3
N_ML_LAYERS = (DEPTH + 1) // 3
N_SC_LAYERS = DEPTH // 3

kernel_name = "hybrid_stickbreak_mlstm_shortconv_deepnorm"


def layer_norm(x, g, b):
    xf = x.astype(jnp.float32)
    mu = jnp.mean(xf, axis=-1, keepdims=True)
    xc = xf - mu
    var = jnp.mean(xc * xc, axis=-1, keepdims=True)
    return (xc * lax.rsqrt(var + LN_EPS) * g + b).astype(x.dtype)


def stick_breaking_attention(x, w_qkv, w_out):
    bsz, seq, d = x.shape
    q, k, v = jnp.split(x @ w_qkv, 3, axis=-1)
    q = q.reshape(bsz, seq, SB_HEADS, SB_HEAD_DIM)
    kf = k.reshape(bsz, seq, SB_HEADS, SB_HEAD_DIM).astype(jnp.float32)
    vf = v.reshape(bsz, seq, SB_HEADS, SB_HEAD_DIM).astype(jnp.float32)
    n_blocks = seq // QBLOCK
    q_blocks = q.reshape(bsz, n_blocks, QBLOCK, SB_HEADS, SB_HEAD_DIM).transpose(1, 0, 2, 3, 4)
    key_pos = jnp.arange(seq)
    scale = 1.0 / math.sqrt(SB_HEAD_DIM)

    def one_block(args):
        q_blk, blk = args
        z = jnp.einsum('bqhd,bkhd->bhqk', q_blk.astype(jnp.float32), kf) * scale
        q_pos = blk * QBLOCK + jnp.arange(QBLOCK)
        strict = key_pos[None, :] < q_pos[:, None]
        log_keep = jnp.where(strict, jax.nn.log_sigmoid(-z), 0.0)
        after = lax.cumsum(log_keep, axis=3, reverse=True) - log_keep
        weights = jnp.where(strict, jnp.exp(jax.nn.log_sigmoid(z) + after), 0.0)
        return jnp.einsum('bhqk,bkhd->bqhd', weights, vf)

    o = lax.map(one_block, (q_blocks, jnp.arange(n_blocks)))
    o = o.transpose(1, 0, 2, 3, 4).reshape(bsz, seq, d).astype(x.dtype)
    return o @ w_out


def mlstm_mixer(x, w_in, b_gates, norm_w, w_out):
    bsz, seq, d = x.shape
    qk = ML_HEADS * ML_QK_DIM
    q, k, v, o_pre, gates = jnp.split(x @ w_in, [qk, 2 * qk, 2 * qk + d, 2 * qk + 2 * d], axis=-1)
    gates = gates.astype(jnp.float32) + b_gates.astype(jnp.float32)
    i_pre = gates[..., :ML_HEADS]
    log_f = jax.nn.log_sigmoid(gates[..., ML_HEADS:])
    n_chunks = seq // CHUNK

    def to_chunks(t, dim):
        return t.astype(jnp.float32).reshape(bsz, n_chunks, CHUNK, ML_HEADS, dim).transpose(1, 0, 3, 2, 4)

    def gate_chunks(g):
        return g.reshape(bsz, n_chunks, CHUNK, ML_HEADS).transpose(1, 0, 3, 2)

    qc = to_chunks(q, ML_QK_DIM) * (ML_QK_DIM ** -0.5)
    kc = to_chunks(k, ML_QK_DIM)
    vc = to_chunks(v, ML_V_DIM)
    ic = gate_chunks(i_pre)
    fc = gate_chunks(log_f)
    tril = jnp.tril(jnp.ones((CHUNK, CHUNK), dtype=bool))

    def chunk_step(carry, inp):
        c_st, n_st, m_st = carry
        qb, kb, vb, ib, fb = inp
        b = jnp.cumsum(fb, axis=-1)
        g = b[..., -1]
        dmat = jnp.where(tril, b[..., :, None] - b[..., None, :] + ib[..., None, :], -jnp.inf)
        inter_log = b + m_st[..., None]
        m_q = jnp.maximum(inter_log, jnp.max(dmat, axis=-1))
        p = jnp.einsum('bhtd,bhsd->bhts', qb, kb) * jnp.exp(dmat - m_q[..., None])
        inter_scale = jnp.exp(inter_log - m_q)
        num = inter_scale[..., None] * jnp.einsum('bhtd,bhdv->bhtv', qb, c_st) + jnp.einsum('bhts,bhsv->bhtv', p, vb)
        den = inter_scale * jnp.einsum('bhtd,bhd->bht', qb, n_st) + jnp.sum(p, axis=-1)
        h = num / jnp.maximum(jnp.abs(den), jnp.exp(-m_q))[..., None]
        key_log = g[..., None] - b + ib
        m_new = jnp.maximum(g + m_st, jnp.max(key_log, axis=-1))
        kw = jnp.exp(key_log - m_new[..., None])
        decay = jnp.exp(g + m_st - m_new)
        c_new = decay[..., None, None] * c_st + jnp.einsum('bhs,bhsd,bhsv->bhdv', kw, kb, vb)
        n_new = decay[..., None] * n_st + jnp.einsum('bhs,bhsd->bhd', kw, kb)
        return (c_new, n_new, m_new), h

    init = (jnp.zeros((bsz, ML_HEADS, ML_QK_DIM, ML_V_DIM), jnp.float32),
            jnp.zeros((bsz, ML_HEADS, ML_QK_DIM), jnp.float32),
            jnp.zeros((bsz, ML_HEADS), jnp.float32))
    _, h = lax.scan(chunk_step, init, (qc, kc, vc, ic, fc))
    h = h.transpose(1, 0, 3, 2, 4).reshape(bsz, seq, ML_HEADS, ML_V_DIM)
    h = h * lax.rsqrt(jnp.mean(h * h, axis=-1, keepdims=True) + HEAD_NORM_EPS)
    h = h.reshape(bsz, seq, d) * norm_w
    y = (jax.nn.sigmoid(o_pre.astype(jnp.float32)) * h).astype(x.dtype)
    return y @ w_out


def short_conv_mixer(x, w_in, conv_w, w_out):
    d = x.shape[-1]
    b_gate, c_gate, u = jnp.split(x @ w_in, 3, axis=-1)
    conv = lax.conv_general_dilated(
        c_gate * u, conv_w[:, None, :], window_strides=(1,),
        padding=[(CONV_WIDTH - 1, 0)], dimension_numbers=('NWC', 'WIO', 'NWC'),
        feature_group_count=d)
    return (b_gate * conv) @ w_out


def squared_relu_mlp(x, w1, w2):
    h = jax.nn.relu(x @ w1)
    return (h * h) @ w2


def setup_inputs(seed: int = 0) -> dict:
    key = jax.random.key(seed)
    ks = jax.random.split(key, 16)
    d = D_MODEL
    f32 = jnp.float32
    nrm = lambda k, shape, s: jax.random.normal(k, shape, f32) * s
    ml_cols = 2 * ML_HEADS * ML_QK_DIM + 2 * d + 2 * ML_HEADS
    i_bias = nrm(ks[5], (N_ML_LAYERS, ML_HEADS), 0.1)
    f_bias = jnp.linspace(3.0, 6.0, ML_HEADS, dtype=f32)[None, :] + nrm(ks[6], (N_ML_LAYERS, ML_HEADS), 0.1)
    return {
        "x": nrm(ks[0], (BATCH, SEQ, d), 1.0),
        "sb_w_qkv": nrm(ks[1], (N_SB_LAYERS, d, 3 * d), d ** -0.5),
        "sb_w_out": nrm(ks[2], (N_SB_LAYERS, d, d), d ** -0.5 * DEEPNORM_BETA),
        "ml_w_in": nrm(ks[3], (N_ML_LAYERS, d, ml_cols), d ** -0.5),
        "ml_b_gates": jnp.concatenate([i_bias, f_bias], axis=-1),
        "ml_norm_w": 1.0 + nrm(ks[7], (N_ML_LAYERS, d), 0.02),
        "ml_w_out": nrm(ks[4], (N_ML_LAYERS, d, d), d ** -0.5 * DEEPNORM_BETA),
        "sc_w_in": nrm(ks[8], (N_SC_LAYERS, d, 3 * d), d ** -0.5),
        "sc_conv_w": nrm(ks[9], (N_SC_LAYERS, CONV_WIDTH, d), CONV_WIDTH ** -0.5),
        "sc_w_out": nrm(ks[10], (N_SC_LAYERS, d, d), d ** -0.5 * DEEPNORM_BETA),
        "mlp_w1": nrm(ks[11], (DEPTH, d, D_FF), d ** -0.5),
        "mlp_w2": nrm(ks[12], (DEPTH, D_FF, d), D_FF ** -0.5 * DEEPNORM_BETA),
        "ln_g": 1.0 + nrm(ks[13], (DEPTH, 2, d), 0.02),
        "ln_b": nrm(ks[14], (DEPTH, 2, d), 0.02),
    }


def reference(x, sb_w_qkv, sb_w_out, ml_w_in, ml_b_gates, ml_norm_w, ml_w_out,
              sc_w_in, sc_conv_w, sc_w_out, mlp_w1, mlp_w2, ln_g, ln_b):
    h = x
    for layer in range(DEPTH):
        kind = layer % N_MIXERS
        slot = layer // N_MIXERS
        if kind == 0:
            mix = stick_breaking_attention(h, sb_w_qkv[slot], sb_w_out[slot])
        elif kind == 1:
            mix = mlstm_mixer(h, ml_w_in[slot], ml_b_gates[slot], ml_norm_w[slot], ml_w_out[slot])
        else:
            mix = short_conv_mixer(h, sc_w_in[slot], sc_conv_w[slot], sc_w_out[slot])
        h = layer_norm(DEEPNORM_ALPHA * h + mix, ln_g[layer, 0], ln_b[layer, 0])
        h = layer_norm(DEEPNORM_ALPHA * h + squared_relu_mlp(h, mlp_w1[layer], mlp_w2[layer]),
                       ln_g[layer, 1], ln_b[layer, 1])
    return h
```

```python
import functools
import math

import jax
import jax.numpy as jnp
from jax import lax
from jax.experimental import pallas as pl
from jax.experimental.pallas import tpu as pltpu

DEPTH = 4
N_MIXERS = 3
SB_HEADS = 32
ML_HEADS = 8
ML_CHUNK = 64
CONV_WIDTH = 3
DEEPNORM_ALPHA = (2 * DEPTH) ** 0.25
LN_EPS = 1e-5
HEAD_NORM_EPS = 1e-6

V7X_LANES = 128
V7X_BF16_SUBLANES = 16
V7X_VMEM_LIMIT_BYTES = 56 * 1024 * 1024

F32 = jnp.float32
BF16 = jnp.bfloat16


def _tile(n, target, align=V7X_LANES):
    if n <= target:
        return n
    t = (target // align) * align
    while n % t:
        t -= align
    return t


def _params(n_axes):
    return pltpu.CompilerParams(
        dimension_semantics=("arbitrary",) * n_axes,
        vmem_limit_bytes=V7X_VMEM_LIMIT_BYTES)


def _mm_kernel(a_ref, b_ref, o_ref, *, act):
    acc = jnp.dot(a_ref[...], b_ref[...], preferred_element_type=F32)
    if act == "relu2":
        r = jnp.maximum(acc, 0.0)
        acc = r * r
    o_ref[...] = acc.astype(o_ref.dtype)


def _matmul(a, b, *, out_dtype, act=None, tm=1024, tn=1024):
    m, k = a.shape
    _, n = b.shape
    tm, tn = _tile(m, tm), _tile(n, tn)
    return pl.pallas_call(
        functools.partial(_mm_kernel, act=act),
        grid=(m // tm, n // tn),
        in_specs=[pl.BlockSpec((tm, k), lambda i, j: (i, 0)),
                  pl.BlockSpec((k, tn), lambda i, j: (0, j))],
        out_specs=pl.BlockSpec((tm, tn), lambda i, j: (i, j)),
        out_shape=jax.ShapeDtypeStruct((m, n), out_dtype),
        compiler_params=_params(2),
        name="matmul_fullk",
    )(a, b)


def _mm_ksplit_kernel(a_ref, b_ref, o_ref, acc_ref):
    kk = pl.program_id(2)

    @pl.when(kk == 0)
    def _():
        acc_ref[...] = jnp.zeros_like(acc_ref)

    acc_ref[...] += jnp.dot(a_ref[...], b_ref[...], preferred_element_type=F32)

    @pl.when(kk == pl.num_programs(2) - 1)
    def _():
        o_ref[...] = acc_ref[...].astype(o_ref.dtype)


def _matmul_ksplit(a, b, *, out_dtype, tm=1024, tn=1024, tk=2048):
    m, k = a.shape
    _, n = b.shape
    tm, tn, tk = _tile(m, tm), _tile(n, tn), _tile(k, tk)
    return pl.pallas_call(
        _mm_ksplit_kernel,
        grid=(m // tm, n // tn, k // tk),
        in_specs=[pl.BlockSpec((tm, tk), lambda i, j, kk: (i, kk)),
                  pl.BlockSpec((tk, tn), lambda i, j, kk: (kk, j))],
        out_specs=pl.BlockSpec((tm, tn), lambda i, j, kk: (i, j)),
        out_shape=jax.ShapeDtypeStruct((m, n), out_dtype),
        scratch_shapes=[pltpu.VMEM((tm, tn), F32)],
        compiler_params=_params(3),
        name="matmul_ksplit",
    )(a, b)


def _ln_kernel(h_ref, mix_ref, g_ref, b_ref, o_ref, o16_ref):
    xf = DEEPNORM_ALPHA * h_ref[...] + mix_ref[...]
    mu = jnp.mean(xf, axis=-1, keepdims=True)
    xc = xf - mu
    var = jnp.mean(xc * xc, axis=-1, keepdims=True)
    y = xc * lax.rsqrt(var + LN_EPS) * g_ref[...] + b_ref[...]
    o_ref[...] = y
    o16_ref[...] = y.astype(BF16)


def _residual_ln(h, mix, g, b, *, tm=256):
    m, d = h.shape
    tm = _tile(m, tm, 8)
    row = pl.BlockSpec((tm, d), lambda i: (i, 0))
    vec = pl.BlockSpec((1, d), lambda i: (0, 0))
    return pl.pallas_call(
        _ln_kernel,
        grid=(m // tm,),
        in_specs=[row, row, vec, vec],
        out_specs=[row, row],
        out_shape=[jax.ShapeDtypeStruct((m, d), F32), jax.ShapeDtypeStruct((m, d), BF16)],
        compiler_params=_params(1),
        name="residual_layernorm",
    )(h, mix, g.reshape(1, d), b.reshape(1, d))


SB_TQ = 256
SB_TK = 128


def _sb_half(z, sum_mat, r_prev, strict):
    sp = jnp.maximum(z, 0.0) + jnp.log(1.0 + jnp.exp(-jnp.abs(z)))
    if strict is not None:
        sp = jnp.where(strict, sp, 0.0)
    hi = sp.astype(BF16)
    lo = (sp - hi.astype(F32)).astype(BF16)
    sums = jnp.dot(jnp.concatenate([hi, lo], axis=1), sum_mat, preferred_element_type=F32)
    suffix = sums[:, :SB_TK]
    total = sums[:, SB_TK:]
    w = jnp.exp(z - sp - suffix - r_prev)
    if strict is not None:
        w = jnp.where(strict, w, 0.0)
    return w, r_prev + total


def _sb_group(q, k_ref, v_ref, sum_mat, key0, acc_ref, r_ref, masks):
    kb = k_ref[pl.ds(key0, 2 * SB_TK), :]
    vb = v_ref[pl.ds(key0, 2 * SB_TK), :]
    z = lax.dot_general(q, kb, (((1,), (1,)), ((), ())), preferred_element_type=F32)
    r0 = r_ref[...]
    w_hi, r1 = _sb_half(z[:, SB_TK:], sum_mat, r0, None if masks is None else masks[1])
    w_lo, r2 = _sb_half(z[:, :SB_TK], sum_mat, r1, None if masks is None else masks[0])
    w = jnp.concatenate([w_lo, w_hi], axis=1).astype(BF16)
    acc_ref[...] += jnp.dot(w, vb, preferred_element_type=F32)
    r_ref[...] = r2


def _sb_kernel(q_ref, k_ref, v_ref, sum_ref, o_ref, acc_ref, r_ref, *, scale):
    i = pl.program_id(1)
    q = (q_ref[...].astype(F32) * scale).astype(BF16)
    sum_mat = sum_ref[...]
    acc_ref[...] = jnp.zeros_like(acc_ref)
    r_ref[...] = jnp.zeros_like(r_ref)

    row = lax.broadcasted_iota(jnp.int32, (SB_TQ, SB_TK), 0)
    col = lax.broadcasted_iota(jnp.int32, (SB_TQ, SB_TK), 1)
    masks = (col < row, (col + SB_TK) < row)
    q0 = pl.multiple_of(i * SB_TQ, SB_TQ)
    _sb_group(q, k_ref, v_ref, sum_mat, q0, acc_ref, r_ref, masks)

    def body(it, carry):
        key0 = pl.multiple_of((i - 1 - it) * SB_TQ, SB_TQ)
        _sb_group(q, k_ref, v_ref, sum_mat, key0, acc_ref, r_ref, None)
        return carry

    lax.fori_loop(0, i, body, 0)
    o_ref[...] = acc_ref[...].astype(o_ref.dtype)


def _sb_sum_matrix():
    j = lax.broadcasted_iota(jnp.int32, (2 * SB_TK, 2 * SB_TK), 0) % SB_TK
    s = lax.broadcasted_iota(jnp.int32, (2 * SB_TK, 2 * SB_TK), 1)
    return jnp.where(s < SB_TK, j > s, True).astype(BF16)


def _sb_attention(qkv, *, heads):
    seq, d3 = qkv.shape
    d = d3 // 3
    dh = d // heads
    assert dh == V7X_LANES and SB_TQ == 2 * SB_TK and seq % SB_TQ == 0
    return pl.pallas_call(
        functools.partial(_sb_kernel, scale=1.0 / math.sqrt(dh)),
        grid=(heads, seq // SB_TQ),
        in_specs=[pl.BlockSpec((SB_TQ, dh), lambda h, i: (i, h)),
                  pl.BlockSpec((seq, dh), lambda h, i: (0, heads + h)),
                  pl.BlockSpec((seq, dh), lambda h, i: (0, 2 * heads + h)),
                  pl.BlockSpec((2 * SB_TK, 2 * SB_TK), lambda h, i: (0, 0))],
        out_specs=pl.BlockSpec((SB_TQ, dh), lambda h, i: (i, h)),
        out_shape=jax.ShapeDtypeStruct((seq, d), BF16),
        scratch_shapes=[pltpu.VMEM((SB_TQ, dh), F32), pltpu.VMEM((SB_TQ, SB_TK), F32)],
        compiler_params=_params(2),
        name="stickbreak_attention",
    )(qkv, qkv, qkv, _sb_sum_matrix())


def _gates_kernel(x_ref, w_ref, b_ref, o_ref):
    x = x_ref[...]
    w = w_ref[...]
    xh = x.astype(BF16)
    xl = (x - xh.astype(F32)).astype(BF16)
    wh = w.astype(BF16)
    wl = (w - wh.astype(F32)).astype(BF16)
    acc = jnp.dot(xh, wh, preferred_element_type=F32)
    acc += jnp.dot(xh, wl, preferred_element_type=F32)
    acc += jnp.dot(xl, wh, preferred_element_type=F32)
    o_ref[...] = acc + b_ref[...]


def _ml_gates(x, w_pad, b_pad, *, tm=512):
    m, d = x.shape
    n = w_pad.shape[1]
    tm = _tile(m, tm, 8)
    return pl.pallas_call(
        _gates_kernel,
        grid=(m // tm,),
        in_specs=[pl.BlockSpec((tm, d), lambda i: (i, 0)),
                  pl.BlockSpec((d, n), lambda i: (0, 0)),
                  pl.BlockSpec((1, n), lambda i: (0, 0))],
        out_specs=pl.BlockSpec((tm, n), lambda i: (i, 0)),
        out_shape=jax.ShapeDtypeStruct((m, n), F32),
        compiler_params=_params(1),
        name="mlstm_gates",
    )(x, w_pad, b_pad)


def _mlstm_kernel(q_ref, k_ref, v_ref, o_ref, g_ref, nw_ref, y_ref, c_ref, m_ref, *, heads, dqk, dv):
    L = ML_CHUNK
    dva = dv + V7X_LANES

    @pl.when(pl.program_id(0) == 0)
    def _():
        c_ref[...] = jnp.zeros_like(c_ref)
        m_ref[...] = jnp.zeros_like(m_ref)

    r = lax.broadcasted_iota(jnp.int32, (L, L), 0)
    c = lax.broadcasted_iota(jnp.int32, (L, L), 1)
    eye = r == c
    causal = c <= r
    ones_col = (lax.broadcasted_iota(jnp.int32, (L, V7X_LANES), 1) == 0).astype(BF16)
    gates = g_ref[...]

    for h in range(heads):
        i_col = gates[:, h:h + 1]
        f_pre = gates[:, heads + h:heads + h + 1]
        lf_col = jnp.minimum(f_pre, 0.0) - jnp.log(1.0 + jnp.exp(-jnp.abs(f_pre)))
        lf_row = jnp.sum(jnp.where(eye, lf_col, 0.0), axis=0, keepdims=True)
        i_row = jnp.sum(jnp.where(eye, i_col, 0.0), axis=0, keepdims=True)
        b_col = jnp.sum(jnp.where(causal, lf_row, 0.0), axis=1, keepdims=True)
        b_row = jnp.sum(jnp.where(r <= c, lf_col, 0.0), axis=0, keepdims=True)
        g = jnp.sum(lf_col, axis=0, keepdims=True)
        m_st = m_ref[h][0:1, 0:1]

        dmat = jnp.where(causal, b_col - b_row + i_row, -jnp.inf)
        inter_log = b_col + m_st
        m_q = jnp.maximum(inter_log, jnp.max(dmat, axis=1, keepdims=True))
        pexp = jnp.exp(dmat - m_q)
        inter_scale = jnp.exp(inter_log - m_q)

        qh = q_ref[:, h * dqk:(h + 1) * dqk] * (dqk ** -0.5)
        kh = k_ref[:, h * dqk:(h + 1) * dqk]
        vh = jnp.concatenate([v_ref[:, h * dv:(h + 1) * dv], ones_col], axis=1)
        qk = lax.dot_general(qh, kh, (((1,), (1,)), ((), ())), preferred_element_type=F32)
        p = (qk * pexp).astype(BF16)
        c_st = c_ref[h]
        tot = inter_scale * jnp.dot(qh, c_st.astype(BF16), preferred_element_type=F32)
        tot += jnp.dot(p, vh, preferred_element_type=F32)
        num = tot[:, :dv]
        den = tot[:, dv:dv + 1]
        hh = num / jnp.maximum(jnp.abs(den), jnp.exp(-m_q))

        key_log = g - b_col + i_col
        m_new = jnp.maximum(g + m_st, jnp.max(key_log, axis=0, keepdims=True))
        kw = jnp.exp(key_log - m_new)
        decay = jnp.exp(g + m_st - m_new)
        kwk = (kw * kh.astype(F32)).astype(BF16)
        upd = lax.dot_general(kwk, vh, (((0,), (0,)), ((), ())), preferred_element_type=F32)
        c_ref[h] = decay * c_st + upd
        m_ref[h] = jnp.broadcast_to(m_new, m_ref.shape[1:])

        hn = hh * lax.rsqrt(jnp.mean(hh * hh, axis=-1, keepdims=True) + HEAD_NORM_EPS)
        hn = hn * nw_ref[:, h * dv:(h + 1) * dv]
        og = o_ref[:, h * dv:(h + 1) * dv].astype(F32)
        y_ref[:, h * dv:(h + 1) * dv] = (hn / (1.0 + jnp.exp(-og))).astype(y_ref.dtype)


def _mlstm(qkvo, gates, norm_w, *, heads):
    seq, cols = qkvo.shape
    d = norm_w.shape[-1]
    dv = d // heads
    dqk = (cols - 2 * d) // (2 * heads)
    qk_w = heads * dqk
    assert (2 * qk_w) % d == 0 and seq % ML_CHUNK == 0
    L = ML_CHUNK
    return pl.pallas_call(
        functools.partial(_mlstm_kernel, heads=heads, dqk=dqk, dv=dv),
        grid=(seq // L,),
        in_specs=[pl.BlockSpec((L, qk_w), lambda t: (t, 0)),
                  pl.BlockSpec((L, qk_w), lambda t: (t, 1)),
                  pl.BlockSpec((L, d), lambda t: (t, (2 * qk_w) // d)),
                  pl.BlockSpec((L, d), lambda t: (t, (2 * qk_w) // d + 1)),
                  pl.BlockSpec((L, gates.shape[1]), lambda t: (t, 0)),
                  pl.BlockSpec((1, d), lambda t: (0, 0))],
        out_specs=pl.BlockSpec((L, d), lambda t: (t, 0)),
        out_shape=jax.ShapeDtypeStruct((seq, d), BF16),
        scratch_shapes=[pltpu.VMEM((heads, dqk, dv + V7X_LANES), F32),
                        pltpu.VMEM((heads, 8, V7X_LANES), F32)],
        compiler_params=_params(1),
        name="mlstm_chunkwise",
    )(qkvo, qkvo, qkvo, qkvo, gates, norm_w.reshape(1, d))


def _conv_kernel(b_ref, c_ref, u_ref, cp_ref, up_ref, w_ref, y_ref):
    i = pl.program_id(0)
    cu = c_ref[...].astype(F32) * u_ref[...].astype(F32)
    halo = cp_ref[...].astype(F32) * up_ref[...].astype(F32)
    halo = jnp.where(i > 0, halo, 0.0)
    hl = halo.shape[0]
    prev1 = halo[hl - 1:hl, :]
    prev2 = halo[hl - 2:hl - 1, :]
    row = lax.broadcasted_iota(jnp.int32, cu.shape, 0)
    s1 = jnp.where(row == 0, prev1, pltpu.roll(cu, 1, axis=0))
    s2 = jnp.where(row == 0, prev2, jnp.where(row == 1, prev1, pltpu.roll(cu, 2, axis=0)))
    w = w_ref[...]
    conv = w[0:1, :] * s2 + w[1:2, :] * s1 + w[2:3, :] * cu
    y_ref[...] = (b_ref[...].astype(F32) * conv).astype(y_ref.dtype)


def _short_conv(bcu, conv_w, *, tm=512, tn=1024):
    seq, d3 = bcu.shape
    d = d3 // 3
    tm, tn = _tile(seq, tm, V7X_BF16_SUBLANES), _tile(d, tn)
    nb = d // tn
    hl = V7X_BF16_SUBLANES
    rb = tm // hl

    def halo_map(off):
        return lambda i, j: (jnp.maximum(i * rb - 1, 0), off + j)

    return pl.pallas_call(
        _conv_kernel,
        grid=(seq // tm, nb),
        in_specs=[pl.BlockSpec((tm, tn), lambda i, j: (i, j)),
                  pl.BlockSpec((tm, tn), lambda i, j: (i, nb + j)),
                  pl.BlockSpec((tm, tn), lambda i, j: (i, 2 * nb + j)),
                  pl.BlockSpec((hl, tn), halo_map(nb)),
                  pl.BlockSpec((hl, tn), halo_map(2 * nb)),
                  pl.BlockSpec((CONV_WIDTH, tn), lambda i, j: (0, j))],
        out_specs=pl.BlockSpec((tm, tn), lambda i, j: (i, j)),
        out_shape=jax.ShapeDtypeStruct((seq, d), BF16),
        compiler_params=_params(2),
        name="gated_short_conv",
    )(bcu, bcu, bcu, bcu, bcu, conv_w)


def kernel(x, sb_w_qkv, sb_w_out, ml_w_in, ml_b_gates, ml_norm_w, ml_w_out,
           sc_w_in, sc_conv_w, sc_w_out, mlp_w1, mlp_w2, ln_g, ln_b):
    bsz, seq, d = x.shape
    outs = []
    for bi in range(bsz):
        h = x[bi]
        h16 = h.astype(BF16)
        for layer in range(DEPTH):
            kind = layer % N_MIXERS
            slot = layer // N_MIXERS
            if kind == 0:
                qkv = _matmul(h16, sb_w_qkv[slot].astype(BF16), out_dtype=BF16)
                y = _sb_attention(qkv, heads=SB_HEADS)
                w_out = sb_w_out[slot]
            elif kind == 1:
                n_main = ml_w_in.shape[-1] - 2 * ML_HEADS
                w_in = ml_w_in[slot]
                qkvo = _matmul(h16, w_in[:, :n_main].astype(BF16), out_dtype=BF16)
                pad = V7X_LANES - 2 * ML_HEADS
                w_g = jnp.pad(w_in[:, n_main:], ((0, 0), (0, pad)))
                b_g = jnp.pad(ml_b_gates[slot].reshape(1, -1), ((0, 0), (0, pad)))
                gates = _ml_gates(h, w_g, b_g)
                y = _mlstm(qkvo, gates, ml_norm_w[slot], heads=ML_HEADS)
                w_out = ml_w_out[slot]
            else:
                bcu = _matmul(h16, sc_w_in[slot].astype(BF16), out_dtype=BF16)
                y = _short_conv(bcu, sc_conv_w[slot])
                w_out = sc_w_out[slot]
            mix = _matmul(y, w_out.astype(BF16), out_dtype=F32)
            h, h16 = _residual_ln(h, mix, ln_g[layer, 0], ln_b[layer, 0])
            mid = _matmul(h16, mlp_w1[layer].astype(BF16), out_dtype=BF16, act="relu2")
            ff = _matmul_ksplit(mid, mlp_w2[layer].astype(BF16), out_dtype=F32)
            h, h16 = _residual_ln(h, ff, ln_g[layer, 1], ln_b[layer, 1])
        outs.append(h)
    return jnp.stack(outs, axis=0)
```

```python
import functools
import math

import jax
import jax.numpy as jnp
from jax import lax
from jax.experimental import pallas as pl
from jax.experimental.pallas import tpu as pltpu

DEPTH = 4
N_MIXERS = 3
SB_HEADS = 32
ML_HEADS = 8
ML_CHUNK = 64
CONV_WIDTH = 3
DEEPNORM_ALPHA = (2 * DEPTH) ** 0.25
LN_EPS = 1e-5
HEAD_NORM_EPS = 1e-6

V7X_LANES = 128
V7X_BF16_SUBLANES = 16
V7X_VMEM_LIMIT_BYTES = 56 * 1024 * 1024

F32 = jnp.float32
BF16 = jnp.bfloat16


def _tile(n, target, align=V7X_LANES):
    if n <= target:
        return n
    t = (target // align) * align
    while n % t:
        t -= align
    return t


def _params(n_axes, flags=None):
    return pltpu.CompilerParams(
        dimension_semantics=("arbitrary",) * n_axes,
        vmem_limit_bytes=V7X_VMEM_LIMIT_BYTES,
        flags=flags)


def _mm_kernel(a_ref, w_ref, o_ref, *, act):
    acc = jnp.dot(a_ref[...], w_ref[...].astype(BF16), preferred_element_type=F32)
    if act == "relu2":
        r = jnp.maximum(acc, 0.0)
        acc = r * r
    o_ref[...] = acc.astype(o_ref.dtype)


def _matmul(a, w_stack, layer, *, out_dtype, n=None, act=None, tm=1024, tn=512):
    m, k = a.shape
    n = w_stack.shape[2] if n is None else n
    tm, tn = _tile(m, tm), _tile(n, tn)
    return pl.pallas_call(
        functools.partial(_mm_kernel, act=act),
        grid=(m // tm, n // tn),
        in_specs=[pl.BlockSpec((tm, k), lambda i, j: (i, 0)),
                  pl.BlockSpec((None, k, tn), lambda i, j: (layer, 0, j))],
        out_specs=pl.BlockSpec((tm, tn), lambda i, j: (i, j)),
        out_shape=jax.ShapeDtypeStruct((m, n), out_dtype),
        compiler_params=_params(2),
        name="matmul_fullk",
    )(a, w_stack)


def _mm_ksplit_kernel(a_ref, w_ref, o_ref, acc_ref):
    kk = pl.program_id(2)

    @pl.when(kk == 0)
    def _():
        acc_ref[...] = jnp.zeros_like(acc_ref)

    acc_ref[...] += jnp.dot(a_ref[...], w_ref[...].astype(BF16), preferred_element_type=F32)

    @pl.when(kk == pl.num_programs(2) - 1)
    def _():
        o_ref[...] = acc_ref[...].astype(o_ref.dtype)


def _matmul_ksplit(a, w_stack, layer, *, out_dtype, tm=1024, tn=1024, tk=2048):
    m, k = a.shape
    n = w_stack.shape[2]
    tm, tn, tk = _tile(m, tm), _tile(n, tn), _tile(k, tk)
    return pl.pallas_call(
        _mm_ksplit_kernel,
        grid=(m // tm, n // tn, k // tk),
        in_specs=[pl.BlockSpec((tm, tk), lambda i, j, kk: (i, kk)),
                  pl.BlockSpec((None, tk, tn), lambda i, j, kk: (layer, kk, j))],
        out_specs=pl.BlockSpec((tm, tn), lambda i, j, kk: (i, j)),
        out_shape=jax.ShapeDtypeStruct((m, n), out_dtype),
        scratch_shapes=[pltpu.VMEM((tm, tn), F32)],
        compiler_params=_params(3),
        name="matmul_ksplit",
    )(a, w_stack)


def _ln_kernel(h_ref, mix_ref, g_ref, b_ref, o_ref, o16_ref):
    xf = DEEPNORM_ALPHA * h_ref[...] + mix_ref[...]
    mu = jnp.mean(xf, axis=-1, keepdims=True)
    xc = xf - mu
    var = jnp.mean(xc * xc, axis=-1, keepdims=True)
    y = xc * lax.rsqrt(var + LN_EPS) * g_ref[...] + b_ref[...]
    o_ref[...] = y
    o16_ref[...] = y.astype(BF16)


def _residual_ln(h, mix, g, b, *, tm=256):
    m, d = h.shape
    tm = _tile(m, tm, 8)
    row = pl.BlockSpec((tm, d), lambda i: (i, 0))
    vec = pl.BlockSpec((1, d), lambda i: (0, 0))
    return pl.pallas_call(
        _ln_kernel,
        grid=(m // tm,),
        in_specs=[row, row, vec, vec],
        out_specs=[row, row],
        out_shape=[jax.ShapeDtypeStruct((m, d), F32), jax.ShapeDtypeStruct((m, d), BF16)],
        compiler_params=_params(1),
        name="residual_layernorm",
    )(h, mix, g.reshape(1, d), b.reshape(1, d))


SB_TQ = 1024
SB_TK = 128
SB_GROUP = 2 * SB_TK
SB_HEADS_PER_STEP = 4
LOG2_E = 1.4426950408889634


def _sb_group(qs, k_ref, v_ref, sum_mat, key0, acc_ref, r_ref, masks, row0=0):
    dh = V7X_LANES
    for hb, q in enumerate(qs):
        cols = slice(hb * dh, (hb + 1) * dh)
        kb = k_ref[pl.ds(key0, SB_GROUP), cols]
        vb = v_ref[pl.ds(key0, SB_GROUP), cols]
        z = lax.dot_general(q[row0:], kb, (((1,), (1,)), ((), ())), preferred_element_type=F32)
        r = r_ref[row0:, cols]
        ws = [None, None]
        for blk in (1, 0):
            zb = z[:, blk * SB_TK:(blk + 1) * SB_TK]
            neg_abs = lax.bitcast_convert_type(
                lax.bitcast_convert_type(zb, jnp.uint32) | jnp.uint32(0x80000000), F32)
            sp = jnp.maximum(zb, 0.0) + jnp.log2(1.0 + jnp.exp2(neg_abs))
            log_beta = zb - sp
            if masks is not None:
                sp = jnp.where(masks[blk], sp, 0.0)
                log_beta = jnp.where(masks[blk], log_beta, -jnp.inf)
            hi = sp.astype(BF16)
            lo = (sp - hi.astype(F32)).astype(BF16)
            sums = jnp.dot(jnp.concatenate([hi, lo], axis=1), sum_mat, preferred_element_type=F32)
            ws[blk] = jnp.exp2(log_beta - sums[:, :SB_TK] - r)
            r = r + sums[:, SB_TK:]
        w = jnp.concatenate(ws, axis=1).astype(BF16)
        acc_ref[row0:, cols] += jnp.dot(w, vb, preferred_element_type=F32)
        r_ref[row0:, cols] = r


def _sb_kernel(q_ref, k_ref, v_ref, sum_ref, o_ref, acc_ref, r_ref, *, scale, hps):
    i = pl.program_id(1)
    dh = V7X_LANES
    qs = [(q_ref[:, hb * dh:(hb + 1) * dh].astype(F32) * (scale * LOG2_E)).astype(BF16) for hb in range(hps)]
    sum_mat = sum_ref[...]
    acc_ref[...] = jnp.zeros_like(acc_ref)
    r_ref[...] = jnp.zeros_like(r_ref)
    q0 = pl.multiple_of(i * SB_TQ, SB_TQ)

    for d in reversed(range(SB_TQ // SB_GROUP)):
        row0 = d * SB_GROUP
        row = lax.broadcasted_iota(jnp.int32, (SB_TQ - row0, SB_TK), 0)
        col = lax.broadcasted_iota(jnp.int32, (SB_TQ - row0, SB_TK), 1)
        masks = tuple((col + blk * SB_TK) < row for blk in range(2))
        _sb_group(qs, k_ref, v_ref, sum_mat, q0 + row0, acc_ref, r_ref, masks, row0)

    def body(it, carry):
        key0 = pl.multiple_of(q0 - (it + 1) * SB_GROUP, SB_GROUP)
        _sb_group(qs, k_ref, v_ref, sum_mat, key0, acc_ref, r_ref, None)
        return carry

    lax.fori_loop(0, i * (SB_TQ // SB_GROUP), body, 0)
    o_ref[...] = acc_ref[...].astype(o_ref.dtype)


def _sb_sum_matrix():
    j = lax.broadcasted_iota(jnp.int32, (2 * SB_TK, 2 * SB_TK), 0) % SB_TK
    s = lax.broadcasted_iota(jnp.int32, (2 * SB_TK, 2 * SB_TK), 1)
    return jnp.where(s < SB_TK, j > s, True).astype(BF16)


def _sb_attention(qkv, *, heads):
    seq, d3 = qkv.shape
    d = d3 // 3
    dh = d // heads
    hps = min(SB_HEADS_PER_STEP, heads)
    assert dh == V7X_LANES and SB_TQ % SB_GROUP == 0 and seq % SB_TQ == 0 and heads % hps == 0
    hg = heads // hps
    w = hps * dh
    return pl.pallas_call(
        functools.partial(_sb_kernel, scale=1.0 / math.sqrt(dh), hps=hps),
        grid=(hg, seq // SB_TQ),
        in_specs=[pl.BlockSpec((SB_TQ, w), lambda g, i: (i, g)),
                  pl.BlockSpec((seq, w), lambda g, i: (0, hg + g)),
                  pl.BlockSpec((seq, w), lambda g, i: (0, 2 * hg + g)),
                  pl.BlockSpec((2 * SB_TK, 2 * SB_TK), lambda g, i: (0, 0))],
        out_specs=pl.BlockSpec((SB_TQ, w), lambda g, i: (i, g)),
        out_shape=jax.ShapeDtypeStruct((seq, d), BF16),
        scratch_shapes=[pltpu.VMEM((SB_TQ, w), F32),
                        pltpu.VMEM((SB_TQ, w), F32)],
        compiler_params=_params(2),
        name="stickbreak_attention",
    )(qkv, qkv, qkv, _sb_sum_matrix())


def _gates_kernel(x_ref, w_ref, b_ref, o_ref):
    x = x_ref[...]
    w = w_ref[...]
    xh = x.astype(BF16)
    xl = (x - xh.astype(F32)).astype(BF16)
    wh = w.astype(BF16)
    wl = (w - wh.astype(F32)).astype(BF16)
    acc = jnp.dot(xh, wh, preferred_element_type=F32)
    acc += jnp.dot(xh, wl, preferred_element_type=F32)
    acc += jnp.dot(xl, wh, preferred_element_type=F32)
    o_ref[...] = acc + b_ref[...]


def _ml_gates(x, w_pad, b_pad, *, tm=512):
    m, d = x.shape
    n = w_pad.shape[1]
    tm = _tile(m, tm, 8)
    return pl.pallas_call(
        _gates_kernel,
        grid=(m // tm,),
        in_specs=[pl.BlockSpec((tm, d), lambda i: (i, 0)),
                  pl.BlockSpec((d, n), lambda i: (0, 0)),
                  pl.BlockSpec((1, n), lambda i: (0, 0))],
        out_specs=pl.BlockSpec((tm, n), lambda i: (i, 0)),
        out_shape=jax.ShapeDtypeStruct((m, n), F32),
        compiler_params=_params(1),
        name="mlstm_gates",
    )(x, w_pad, b_pad)


def _mlstm_kernel(q_ref, k_ref, v_ref, o_ref, g_ref, nw_ref, y_ref, c_ref, m_ref, *, heads, dqk, dv):
    L = ML_CHUNK
    dva = dv + V7X_LANES

    @pl.when(pl.program_id(0) == 0)
    def _():
        c_ref[...] = jnp.zeros_like(c_ref)
        m_ref[...] = jnp.zeros_like(m_ref)

    r = lax.broadcasted_iota(jnp.int32, (L, L), 0)
    c = lax.broadcasted_iota(jnp.int32, (L, L), 1)
    eye = r == c
    causal = c <= r
    ones_col = (lax.broadcasted_iota(jnp.int32, (L, V7X_LANES), 1) == 0).astype(BF16)
    gates = g_ref[...]

    for h in range(heads):
        i_col = gates[:, h:h + 1]
        f_pre = gates[:, heads + h:heads + h + 1]
        lf_col = jnp.minimum(f_pre, 0.0) - jnp.log(1.0 + jnp.exp(-jnp.abs(f_pre)))
        lf_row = jnp.sum(jnp.where(eye, lf_col, 0.0), axis=0, keepdims=True)
        i_row = jnp.sum(jnp.where(eye, i_col, 0.0), axis=0, keepdims=True)
        b_col = jnp.sum(jnp.where(causal, lf_row, 0.0), axis=1, keepdims=True)
        b_row = jnp.sum(jnp.where(r <= c, lf_col, 0.0), axis=0, keepdims=True)
        g = jnp.sum(lf_col, axis=0, keepdims=True)
        m_st = m_ref[h][0:1, 0:1]

        dmat = jnp.where(causal, b_col - b_row + i_row, -jnp.inf)
        inter_log = b_col + m_st
        m_q = jnp.maximum(inter_log, jnp.max(dmat, axis=1, keepdims=True))
        pexp = jnp.exp(dmat - m_q)
        inter_scale = jnp.exp(inter_log - m_q)

        qh = q_ref[:, h * dqk:(h + 1) * dqk] * (dqk ** -0.5)
        kh = k_ref[:, h * dqk:(h + 1) * dqk]
        vh = jnp.concatenate([v_ref[:, h * dv:(h + 1) * dv], ones_col], axis=1)
        qk = lax.dot_general(qh, kh, (((1,), (1,)), ((), ())), preferred_element_type=F32)
        p = (qk * pexp).astype(BF16)
        c_st = c_ref[h]
        tot = inter_scale * jnp.dot(qh, c_st.astype(BF16), preferred_element_type=F32)
        tot += jnp.dot(p, vh, preferred_element_type=F32)
        num = tot[:, :dv]
        den = tot[:, dv:dv + 1]
        hh = num / jnp.maximum(jnp.abs(den), jnp.exp(-m_q))

        key_log = g - b_col + i_col
        m_new = jnp.maximum(g + m_st, jnp.max(key_log, axis=0, keepdims=True))
        kw = jnp.exp(key_log - m_new)
        decay = jnp.exp(g + m_st - m_new)
        kwk = (kw * kh.astype(F32)).astype(BF16)
        upd = lax.dot_general(kwk, vh, (((0,), (0,)), ((), ())), preferred_element_type=F32)
        c_ref[h] = decay * c_st + upd
        m_ref[h] = jnp.broadcast_to(m_new, m_ref.shape[1:])

        hn = hh * lax.rsqrt(jnp.mean(hh * hh, axis=-1, keepdims=True) + HEAD_NORM_EPS)
        hn = hn * nw_ref[:, h * dv:(h + 1) * dv]
        og = o_ref[:, h * dv:(h + 1) * dv].astype(F32)
        y_ref[:, h * dv:(h + 1) * dv] = (hn / (1.0 + jnp.exp(-og))).astype(y_ref.dtype)


def _mlstm(qkvo, gates, norm_w, *, heads):
    seq, cols = qkvo.shape
    d = norm_w.shape[-1]
    dv = d // heads
    dqk = (cols - 2 * d) // (2 * heads)
    qk_w = heads * dqk
    assert (2 * qk_w) % d == 0 and seq % ML_CHUNK == 0
    L = ML_CHUNK
    return pl.pallas_call(
        functools.partial(_mlstm_kernel, heads=heads, dqk=dqk, dv=dv),
        grid=(seq // L,),
        in_specs=[pl.BlockSpec((L, qk_w), lambda t: (t, 0)),
                  pl.BlockSpec((L, qk_w), lambda t: (t, 1)),
                  pl.BlockSpec((L, d), lambda t: (t, (2 * qk_w) // d)),
                  pl.BlockSpec((L, d), lambda t: (t, (2 * qk_w) // d + 1)),
                  pl.BlockSpec((L, gates.shape[1]), lambda t: (t, 0)),
                  pl.BlockSpec((1, d), lambda t: (0, 0))],
        out_specs=pl.BlockSpec((L, d), lambda t: (t, 0)),
        out_shape=jax.ShapeDtypeStruct((seq, d), BF16),
        scratch_shapes=[pltpu.VMEM((heads, dqk, dv + V7X_LANES), F32),
                        pltpu.VMEM((heads, 8, V7X_LANES), F32)],
        compiler_params=_params(1),
        name="mlstm_chunkwise",
    )(qkvo, qkvo, qkvo, qkvo, gates, norm_w.reshape(1, d))


def _conv_kernel(b_ref, c_ref, u_ref, cp_ref, up_ref, w_ref, y_ref):
    i = pl.program_id(0)
    cu = c_ref[...].astype(F32) * u_ref[...].astype(F32)
    halo = cp_ref[...].astype(F32) * up_ref[...].astype(F32)
    halo = jnp.where(i > 0, halo, 0.0)
    hl = halo.shape[0]
    prev1 = halo[hl - 1:hl, :]
    prev2 = halo[hl - 2:hl - 1, :]
    row = lax.broadcasted_iota(jnp.int32, cu.shape, 0)
    s1 = jnp.where(row == 0, prev1, pltpu.roll(cu, 1, axis=0))
    s2 = jnp.where(row == 0, prev2, jnp.where(row == 1, prev1, pltpu.roll(cu, 2, axis=0)))
    w = w_ref[...]
    conv = w[0:1, :] * s2 + w[1:2, :] * s1 + w[2:3, :] * cu
    y_ref[...] = (b_ref[...].astype(F32) * conv).astype(y_ref.dtype)


def _short_conv(bcu, conv_w, *, tm=512, tn=1024):
    seq, d3 = bcu.shape
    d = d3 // 3
    tm, tn = _tile(seq, tm, V7X_BF16_SUBLANES), _tile(d, tn)
    nb = d // tn
    hl = V7X_BF16_SUBLANES
    rb = tm // hl

    def halo_map(off):
        return lambda i, j: (jnp.maximum(i * rb - 1, 0), off + j)

    return pl.pallas_call(
        _conv_kernel,
        grid=(seq // tm, nb),
        in_specs=[pl.BlockSpec((tm, tn), lambda i, j: (i, j)),
                  pl.BlockSpec((tm, tn), lambda i, j: (i, nb + j)),
                  pl.BlockSpec((tm, tn), lambda i, j: (i, 2 * nb + j)),
                  pl.BlockSpec((hl, tn), halo_map(nb)),
                  pl.BlockSpec((hl, tn), halo_map(2 * nb)),
                  pl.BlockSpec((CONV_WIDTH, tn), lambda i, j: (0, j))],
        out_specs=pl.BlockSpec((tm, tn), lambda i, j: (i, j)),
        out_shape=jax.ShapeDtypeStruct((seq, d), BF16),
        compiler_params=_params(2),
        name="gated_short_conv",
    )(bcu, bcu, bcu, bcu, bcu, conv_w)


def kernel(x, sb_w_qkv, sb_w_out, ml_w_in, ml_b_gates, ml_norm_w, ml_w_out,
           sc_w_in, sc_conv_w, sc_w_out, mlp_w1, mlp_w2, ln_g, ln_b):
    bsz, seq, d = x.shape
    outs = []
    for bi in range(bsz):
        h = x[bi]
        h16 = h.astype(BF16)
        for layer in range(DEPTH):
            kind = layer % N_MIXERS
            slot = layer // N_MIXERS
            if kind == 0:
                qkv = _matmul(h16, sb_w_qkv, slot, out_dtype=BF16)
                y = _sb_attention(qkv, heads=SB_HEADS)
                w_out = sb_w_out
            elif kind == 1:
                n_main = ml_w_in.shape[-1] - 2 * ML_HEADS
                qkvo = _matmul(h16, ml_w_in, slot, n=n_main, out_dtype=BF16)
                pad = V7X_LANES - 2 * ML_HEADS
                w_g = jnp.pad(ml_w_in[slot, :, n_main:], ((0, 0), (0, pad)))
                b_g = jnp.pad(ml_b_gates[slot].reshape(1, -1), ((0, 0), (0, pad)))
                gates = _ml_gates(h, w_g, b_g)
                y = _mlstm(qkvo, gates, ml_norm_w[slot], heads=ML_HEADS)
                w_out = ml_w_out
            else:
                bcu = _matmul(h16, sc_w_in, slot, out_dtype=BF16)
                y = _short_conv(bcu, sc_conv_w[slot])
                w_out = sc_w_out
            mix = _matmul(y, w_out, slot, out_dtype=F32)
            h, h16 = _residual_ln(h, mix, ln_g[layer, 0], ln_b[layer, 0])
            mid = _matmul(h16, mlp_w1, layer, out_dtype=BF16, act="relu2")
            ff = _matmul_ksplit(mid, mlp_w2, layer, out_dtype=F32)
            h, h16 = _residual_ln(h, ff, ln_g[layer, 1], ln_b[layer, 1])
        outs.append(h)
    return jnp.stack(outs, axis=0)
```

```python
import functools
import math

import jax
import jax.numpy as jnp
from jax import lax
from jax.experimental import pallas as pl
from jax.experimental.pallas import tpu as pltpu

DEPTH = 4
N_MIXERS = 3
SB_HEADS = 32
ML_HEADS = 8
ML_CHUNK = 64
CONV_WIDTH = 3
DEEPNORM_ALPHA = (2 * DEPTH) ** 0.25
LN_EPS = 1e-5
HEAD_NORM_EPS = 1e-6

V7X_LANES = 128
V7X_BF16_SUBLANES = 16
V7X_VMEM_LIMIT_BYTES = 56 * 1024 * 1024

F32 = jnp.float32
BF16 = jnp.bfloat16


def _tile(n, target, align=V7X_LANES):
    if n <= target:
        return n
    t = (target // align) * align
    while n % t:
        t -= align
    return t


def _params(n_axes, flags=None):
    return pltpu.CompilerParams(
        dimension_semantics=("arbitrary",) * n_axes,
        vmem_limit_bytes=V7X_VMEM_LIMIT_BYTES,
        flags=flags)


def _mm_kernel(a_ref, w_ref, o_ref, *, act):
    acc = jnp.dot(a_ref[...], w_ref[...].astype(BF16), preferred_element_type=F32)
    if act == "relu2":
        r = jnp.maximum(acc, 0.0)
        acc = r * r
    o_ref[...] = acc.astype(o_ref.dtype)


def _matmul(a, w_stack, layer, *, out_dtype, n=None, act=None, tm=1024, tn=512):
    m, k = a.shape
    n = w_stack.shape[2] if n is None else n
    tm, tn = _tile(m, tm), _tile(n, tn)
    return pl.pallas_call(
        functools.partial(_mm_kernel, act=act),
        grid=(m // tm, n // tn),
        in_specs=[pl.BlockSpec((tm, k), lambda i, j: (i, 0)),
                  pl.BlockSpec((None, k, tn), lambda i, j: (layer, 0, j))],
        out_specs=pl.BlockSpec((tm, tn), lambda i, j: (i, j)),
        out_shape=jax.ShapeDtypeStruct((m, n), out_dtype),
        compiler_params=_params(2),
        name="matmul_fullk",
    )(a, w_stack)


def _mm_ksplit_kernel(a_ref, w_ref, o_ref, acc_ref):
    kk = pl.program_id(2)

    @pl.when(kk == 0)
    def _():
        acc_ref[...] = jnp.zeros_like(acc_ref)

    acc_ref[...] += jnp.dot(a_ref[...], w_ref[...].astype(BF16), preferred_element_type=F32)

    @pl.when(kk == pl.num_programs(2) - 1)
    def _():
        o_ref[...] = acc_ref[...].astype(o_ref.dtype)


def _matmul_ksplit(a, w_stack, layer, *, out_dtype, tm=1024, tn=512, tk=4096):
    m, k = a.shape
    n = w_stack.shape[2]
    tm, tn, tk = _tile(m, tm), _tile(n, tn), _tile(k, tk)
    return pl.pallas_call(
        _mm_ksplit_kernel,
        grid=(m // tm, n // tn, k // tk),
        in_specs=[pl.BlockSpec((tm, tk), lambda i, j, kk: (i, kk)),
                  pl.BlockSpec((None, tk, tn), lambda i, j, kk: (layer, kk, j))],
        out_specs=pl.BlockSpec((tm, tn), lambda i, j, kk: (i, j)),
        out_shape=jax.ShapeDtypeStruct((m, n), out_dtype),
        scratch_shapes=[pltpu.VMEM((tm, tn), F32)],
        compiler_params=_params(3),
        name="matmul_ksplit",
    )(a, w_stack)


def _ln_kernel(h_ref, mix_ref, g_ref, b_ref, o_ref, o16_ref):
    xf = DEEPNORM_ALPHA * h_ref[...] + mix_ref[...].astype(F32)
    mu = jnp.mean(xf, axis=-1, keepdims=True)
    xc = xf - mu
    var = jnp.mean(xc * xc, axis=-1, keepdims=True)
    y = xc * lax.rsqrt(var + LN_EPS) * g_ref[...] + b_ref[...]
    o_ref[...] = y
    o16_ref[...] = y.astype(BF16)


def _residual_ln(h, mix, g, b, *, tm=256):
    m, d = h.shape
    tm = _tile(m, tm, 8)
    row = pl.BlockSpec((tm, d), lambda i: (i, 0))
    vec = pl.BlockSpec((1, d), lambda i: (0, 0))
    return pl.pallas_call(
        _ln_kernel,
        grid=(m // tm,),
        in_specs=[row, row, vec, vec],
        out_specs=[row, row],
        out_shape=[jax.ShapeDtypeStruct((m, d), F32), jax.ShapeDtypeStruct((m, d), BF16)],
        compiler_params=_params(1),
        name="residual_layernorm",
    )(h, mix, g.reshape(1, d), b.reshape(1, d))


SB_TQ = 1024
SB_TK = 128
SB_GROUP = 2 * SB_TK
SB_HEADS_PER_STEP = 4
LOG2_E = 1.4426950408889634


def _sb_group(qs, k_ref, v_ref, sum_mat, key0, acc_ref, r_ref, masks, row0=0):
    dh = V7X_LANES
    for hb, q in enumerate(qs):
        cols = slice(hb * dh, (hb + 1) * dh)
        kb = k_ref[pl.ds(key0, SB_GROUP), cols]
        vb = v_ref[pl.ds(key0, SB_GROUP), cols]
        z = lax.dot_general(q[row0:], kb, (((1,), (1,)), ((), ())), preferred_element_type=F32)
        neg_abs = lax.bitcast_convert_type(
            lax.bitcast_convert_type(z, jnp.uint32) | jnp.uint32(0x80000000), F32)
        sp = jnp.maximum(z, 0.0) + jnp.log2(1.0 + jnp.exp2(neg_abs))
        log_beta = z - sp
        if masks is not None:
            sp = jnp.where(masks, sp, 0.0)
            log_beta = jnp.where(masks, log_beta, -jnp.inf)
        suffix = jnp.dot(sp.astype(BF16), sum_mat, preferred_element_type=F32)
        total = jnp.sum(sp, axis=1, keepdims=True)
        r = r_ref[row0:, cols]
        w = jnp.exp2(log_beta - suffix - jnp.concatenate([r, r], axis=1)).astype(BF16)
        acc_ref[row0:, cols] += jnp.dot(w, vb, preferred_element_type=F32)
        r_ref[row0:, cols] = r + total


def _sb_kernel(q_ref, k_ref, v_ref, sum_ref, o_ref, acc_ref, r_ref, *, scale, hps):
    i = pl.program_id(1)
    dh = V7X_LANES
    qs = [(q_ref[:, hb * dh:(hb + 1) * dh].astype(F32) * (scale * LOG2_E)).astype(BF16) for hb in range(hps)]
    sum_mat = sum_ref[...]
    acc_ref[...] = jnp.zeros_like(acc_ref)
    r_ref[...] = jnp.zeros_like(r_ref)
    q0 = pl.multiple_of(i * SB_TQ, SB_TQ)

    for d in reversed(range(SB_TQ // SB_GROUP)):
        row0 = d * SB_GROUP
        row = lax.broadcasted_iota(jnp.int32, (SB_TQ - row0, SB_GROUP), 0)
        col = lax.broadcasted_iota(jnp.int32, (SB_TQ - row0, SB_GROUP), 1)
        _sb_group(qs, k_ref, v_ref, sum_mat, q0 + row0, acc_ref, r_ref, col < row, row0)

    def body(it, carry):
        key0 = pl.multiple_of(q0 - (it + 1) * SB_GROUP, SB_GROUP)
        _sb_group(qs, k_ref, v_ref, sum_mat, key0, acc_ref, r_ref, None)
        return carry

    lax.fori_loop(0, i * (SB_TQ // SB_GROUP), body, 0)
    o_ref[...] = acc_ref[...].astype(o_ref.dtype)


def _sb_sum_matrix():
    j = lax.broadcasted_iota(jnp.int32, (SB_GROUP, SB_GROUP), 0)
    s = lax.broadcasted_iota(jnp.int32, (SB_GROUP, SB_GROUP), 1)
    return (j > s).astype(BF16)


def _sb_attention(qkv, *, heads):
    seq, d3 = qkv.shape
    d = d3 // 3
    dh = d // heads
    hps = min(SB_HEADS_PER_STEP, heads)
    assert dh == V7X_LANES and SB_TQ % SB_GROUP == 0 and seq % SB_TQ == 0 and heads % hps == 0
    hg = heads // hps
    w = hps * dh
    return pl.pallas_call(
        functools.partial(_sb_kernel, scale=1.0 / math.sqrt(dh), hps=hps),
        grid=(hg, seq // SB_TQ),
        in_specs=[pl.BlockSpec((SB_TQ, w), lambda g, i: (i, g)),
                  pl.BlockSpec((seq, w), lambda g, i: (0, hg + g)),
                  pl.BlockSpec((seq, w), lambda g, i: (0, 2 * hg + g)),
                  pl.BlockSpec((2 * SB_TK, 2 * SB_TK), lambda g, i: (0, 0))],
        out_specs=pl.BlockSpec((SB_TQ, w), lambda g, i: (i, g)),
        out_shape=jax.ShapeDtypeStruct((seq, d), BF16),
        scratch_shapes=[pltpu.VMEM((SB_TQ, w), F32),
                        pltpu.VMEM((SB_TQ, w), F32)],
        compiler_params=_params(2),
        name="stickbreak_attention",
    )(qkv, qkv, qkv, _sb_sum_matrix())


def _gates_kernel(x_ref, w_ref, b_ref, o_ref):
    x = x_ref[...]
    w = w_ref[...]
    xh = x.astype(BF16)
    xl = (x - xh.astype(F32)).astype(BF16)
    wh = w.astype(BF16)
    wl = (w - wh.astype(F32)).astype(BF16)
    acc = jnp.dot(xh, wh, preferred_element_type=F32)
    acc += jnp.dot(xh, wl, preferred_element_type=F32)
    acc += jnp.dot(xl, wh, preferred_element_type=F32)
    o_ref[...] = acc + b_ref[...]


def _ml_gates(x, w_pad, b_pad, *, tm=512):
    m, d = x.shape
    n = w_pad.shape[1]
    tm = _tile(m, tm, 8)
    return pl.pallas_call(
        _gates_kernel,
        grid=(m // tm,),
        in_specs=[pl.BlockSpec((tm, d), lambda i: (i, 0)),
                  pl.BlockSpec((d, n), lambda i: (0, 0)),
                  pl.BlockSpec((1, n), lambda i: (0, 0))],
        out_specs=pl.BlockSpec((tm, n), lambda i: (i, 0)),
        out_shape=jax.ShapeDtypeStruct((m, n), F32),
        compiler_params=_params(1),
        name="mlstm_gates",
    )(x, w_pad, b_pad)


def _mlstm_kernel(q_ref, k_ref, v_ref, o_ref, g_ref, nw_ref, y_ref, c_ref, m_ref, *, heads, dqk, dv):
    L = ML_CHUNK
    dva = dv + V7X_LANES

    @pl.when(pl.program_id(0) == 0)
    def _():
        c_ref[...] = jnp.zeros_like(c_ref)
        m_ref[...] = jnp.zeros_like(m_ref)

    r = lax.broadcasted_iota(jnp.int32, (L, L), 0)
    c = lax.broadcasted_iota(jnp.int32, (L, L), 1)
    eye = r == c
    causal = c <= r
    ones_col = (lax.broadcasted_iota(jnp.int32, (L, V7X_LANES), 1) == 0).astype(BF16)
    gates = g_ref[...]

    for h in range(heads):
        i_col = gates[:, h:h + 1]
        f_pre = gates[:, heads + h:heads + h + 1]
        lf_col = jnp.minimum(f_pre, 0.0) - jnp.log(1.0 + jnp.exp(-jnp.abs(f_pre)))
        lf_row = jnp.sum(jnp.where(eye, lf_col, 0.0), axis=0, keepdims=True)
        i_row = jnp.sum(jnp.where(eye, i_col, 0.0), axis=0, keepdims=True)
        b_col = jnp.sum(jnp.where(causal, lf_row, 0.0), axis=1, keepdims=True)
        b_row = jnp.sum(jnp.where(r <= c, lf_col, 0.0), axis=0, keepdims=True)
        g = jnp.sum(lf_col, axis=0, keepdims=True)
        m_st = m_ref[h][0:1, 0:1]

        dmat = jnp.where(causal, b_col - b_row + i_row, -jnp.inf)
        inter_log = b_col + m_st
        m_q = jnp.maximum(inter_log, jnp.max(dmat, axis=1, keepdims=True))
        pexp = jnp.exp(dmat - m_q)
        inter_scale = jnp.exp(inter_log - m_q)

        qh = q_ref[:, h * dqk:(h + 1) * dqk] * (dqk ** -0.5)
        kh = k_ref[:, h * dqk:(h + 1) * dqk]
        vh = jnp.concatenate([v_ref[:, h * dv:(h + 1) * dv], ones_col], axis=1)
        qk = lax.dot_general(qh, kh, (((1,), (1,)), ((), ())), preferred_element_type=F32)
        p = (qk * pexp).astype(BF16)
        c_st = c_ref[h]
        tot = inter_scale * jnp.dot(qh, c_st.astype(BF16), preferred_element_type=F32)
        tot += jnp.dot(p, vh, preferred_element_type=F32)
        num = tot[:, :dv]
        den = tot[:, dv:dv + 1]
        hh = num / jnp.maximum(jnp.abs(den), jnp.exp(-m_q))

        key_log = g - b_col + i_col
        m_new = jnp.maximum(g + m_st, jnp.max(key_log, axis=0, keepdims=True))
        kw = jnp.exp(key_log - m_new)
        decay = jnp.exp(g + m_st - m_new)
        kwk = (kw * kh.astype(F32)).astype(BF16)
        upd = lax.dot_general(kwk, vh, (((0,), (0,)), ((), ())), preferred_element_type=F32)
        c_ref[h] = decay * c_st + upd
        m_ref[h] = jnp.broadcast_to(m_new, m_ref.shape[1:])

        hn = hh * lax.rsqrt(jnp.mean(hh * hh, axis=-1, keepdims=True) + HEAD_NORM_EPS)
        hn = hn * nw_ref[:, h * dv:(h + 1) * dv]
        og = o_ref[:, h * dv:(h + 1) * dv].astype(F32)
        y_ref[:, h * dv:(h + 1) * dv] = (hn / (1.0 + jnp.exp(-og))).astype(y_ref.dtype)


def _mlstm(qkvo, gates, norm_w, *, heads):
    seq, cols = qkvo.shape
    d = norm_w.shape[-1]
    dv = d // heads
    dqk = (cols - 2 * d) // (2 * heads)
    qk_w = heads * dqk
    assert (2 * qk_w) % d == 0 and seq % ML_CHUNK == 0
    L = ML_CHUNK
    return pl.pallas_call(
        functools.partial(_mlstm_kernel, heads=heads, dqk=dqk, dv=dv),
        grid=(seq // L,),
        in_specs=[pl.BlockSpec((L, qk_w), lambda t: (t, 0)),
                  pl.BlockSpec((L, qk_w), lambda t: (t, 1)),
                  pl.BlockSpec((L, d), lambda t: (t, (2 * qk_w) // d)),
                  pl.BlockSpec((L, d), lambda t: (t, (2 * qk_w) // d + 1)),
                  pl.BlockSpec((L, gates.shape[1]), lambda t: (t, 0)),
                  pl.BlockSpec((1, d), lambda t: (0, 0))],
        out_specs=pl.BlockSpec((L, d), lambda t: (t, 0)),
        out_shape=jax.ShapeDtypeStruct((seq, d), BF16),
        scratch_shapes=[pltpu.VMEM((heads, dqk, dv + V7X_LANES), F32),
                        pltpu.VMEM((heads, 8, V7X_LANES), F32)],
        compiler_params=_params(1),
        name="mlstm_chunkwise",
    )(qkvo, qkvo, qkvo, qkvo, gates, norm_w.reshape(1, d))


def _conv_kernel(b_ref, c_ref, u_ref, cp_ref, up_ref, w_ref, y_ref):
    i = pl.program_id(0)
    cu = c_ref[...].astype(F32) * u_ref[...].astype(F32)
    halo = cp_ref[...].astype(F32) * up_ref[...].astype(F32)
    halo = jnp.where(i > 0, halo, 0.0)
    hl = halo.shape[0]
    prev1 = halo[hl - 1:hl, :]
    prev2 = halo[hl - 2:hl - 1, :]
    row = lax.broadcasted_iota(jnp.int32, cu.shape, 0)
    s1 = jnp.where(row == 0, prev1, pltpu.roll(cu, 1, axis=0))
    s2 = jnp.where(row == 0, prev2, jnp.where(row == 1, prev1, pltpu.roll(cu, 2, axis=0)))
    w = w_ref[...]
    conv = w[0:1, :] * s2 + w[1:2, :] * s1 + w[2:3, :] * cu
    y_ref[...] = (b_ref[...].astype(F32) * conv).astype(y_ref.dtype)


def _short_conv(bcu, conv_w, *, tm=512, tn=1024):
    seq, d3 = bcu.shape
    d = d3 // 3
    tm, tn = _tile(seq, tm, V7X_BF16_SUBLANES), _tile(d, tn)
    nb = d // tn
    hl = V7X_BF16_SUBLANES
    rb = tm // hl

    def halo_map(off):
        return lambda i, j: (jnp.maximum(i * rb - 1, 0), off + j)

    return pl.pallas_call(
        _conv_kernel,
        grid=(seq // tm, nb),
        in_specs=[pl.BlockSpec((tm, tn), lambda i, j: (i, j)),
                  pl.BlockSpec((tm, tn), lambda i, j: (i, nb + j)),
                  pl.BlockSpec((tm, tn), lambda i, j: (i, 2 * nb + j)),
                  pl.BlockSpec((hl, tn), halo_map(nb)),
                  pl.BlockSpec((hl, tn), halo_map(2 * nb)),
                  pl.BlockSpec((CONV_WIDTH, tn), lambda i, j: (0, j))],
        out_specs=pl.BlockSpec((tm, tn), lambda i, j: (i, j)),
        out_shape=jax.ShapeDtypeStruct((seq, d), BF16),
        compiler_params=_params(2),
        name="gated_short_conv",
    )(bcu, bcu, bcu, bcu, bcu, conv_w)


def kernel(x, sb_w_qkv, sb_w_out, ml_w_in, ml_b_gates, ml_norm_w, ml_w_out,
           sc_w_in, sc_conv_w, sc_w_out, mlp_w1, mlp_w2, ln_g, ln_b):
    bsz, seq, d = x.shape
    outs = []
    for bi in range(bsz):
        h = x[bi]
        h16 = h.astype(BF16)
        for layer in range(DEPTH):
            kind = layer % N_MIXERS
            slot = layer // N_MIXERS
            if kind == 0:
                qkv = _matmul(h16, sb_w_qkv, slot, out_dtype=BF16)
                y = _sb_attention(qkv, heads=SB_HEADS)
                w_out = sb_w_out
            elif kind == 1:
                n_main = ml_w_in.shape[-1] - 2 * ML_HEADS
                qkvo = _matmul(h16, ml_w_in, slot, n=n_main, out_dtype=BF16)
                pad = V7X_LANES - 2 * ML_HEADS
                w_g = jnp.pad(ml_w_in[slot, :, n_main:], ((0, 0), (0, pad)))
                b_g = jnp.pad(ml_b_gates[slot].reshape(1, -1), ((0, 0), (0, pad)))
                gates = _ml_gates(h, w_g, b_g)
                y = _mlstm(qkvo, gates, ml_norm_w[slot], heads=ML_HEADS)
                w_out = ml_w_out
            else:
                bcu = _matmul(h16, sc_w_in, slot, out_dtype=BF16)
                y = _short_conv(bcu, sc_conv_w[slot])
                w_out = sc_w_out
            mix = _matmul(y, w_out, slot, out_dtype=BF16)
            h, h16 = _residual_ln(h, mix, ln_g[layer, 0], ln_b[layer, 0])
            mid = _matmul(h16, mlp_w1, layer, out_dtype=BF16, act="relu2")
            ff = _matmul_ksplit(mid, mlp_w2, layer, out_dtype=BF16)
            h, h16 = _residual_ln(h, ff, ln_g[layer, 1], ln_b[layer, 1])
        outs.append(h)
    return outs[0][None] if bsz == 1 else jnp.stack(outs, axis=0)
```

```python
import functools
import math

import jax
import jax.numpy as jnp
from jax import lax
from jax.experimental import pallas as pl
from jax.experimental.pallas import tpu as pltpu

DEPTH = 4
N_MIXERS = 3
SB_HEADS = 32
ML_HEADS = 8
ML_CHUNK = 64
CONV_WIDTH = 3
DEEPNORM_ALPHA = (2 * DEPTH) ** 0.25
LN_EPS = 1e-5
HEAD_NORM_EPS = 1e-6

V7X_LANES = 128
V7X_BF16_SUBLANES = 16
V7X_VMEM_LIMIT_BYTES = 56 * 1024 * 1024

F32 = jnp.float32
BF16 = jnp.bfloat16


def _tile(n, target, align=V7X_LANES):
    if n <= target:
        return n
    t = (target // align) * align
    while n % t:
        t -= align
    return t


def _params(n_axes, flags=None):
    return pltpu.CompilerParams(
        dimension_semantics=("arbitrary",) * n_axes,
        vmem_limit_bytes=V7X_VMEM_LIMIT_BYTES,
        flags=flags)


def _mm_kernel(a_ref, w_ref, o_ref, *, act):
    acc = jnp.dot(a_ref[...], w_ref[...].astype(BF16), preferred_element_type=F32)
    if act == "relu2":
        r = jnp.maximum(acc, 0.0)
        acc = r * r
    o_ref[...] = acc.astype(o_ref.dtype)


def _matmul(a, w_stack, layer, *, out_dtype, n=None, act=None, tm=1024, tn=512):
    m, k = a.shape
    n = w_stack.shape[2] if n is None else n
    tm, tn = _tile(m, tm), _tile(n, tn)
    return pl.pallas_call(
        functools.partial(_mm_kernel, act=act),
        grid=(m // tm, n // tn),
        in_specs=[pl.BlockSpec((tm, k), lambda i, j: (i, 0)),
                  pl.BlockSpec((None, k, tn), lambda i, j: (layer, 0, j))],
        out_specs=pl.BlockSpec((tm, tn), lambda i, j: (i, j)),
        out_shape=jax.ShapeDtypeStruct((m, n), out_dtype),
        compiler_params=_params(2),
        name="matmul_fullk",
    )(a, w_stack)


def _mm_ksplit_kernel(a_ref, w_ref, o_ref, acc_ref):
    kk = pl.program_id(2)

    @pl.when(kk == 0)
    def _():
        acc_ref[...] = jnp.zeros_like(acc_ref)

    acc_ref[...] += jnp.dot(a_ref[...], w_ref[...].astype(BF16), preferred_element_type=F32)

    @pl.when(kk == pl.num_programs(2) - 1)
    def _():
        o_ref[...] = acc_ref[...].astype(o_ref.dtype)


def _matmul_ksplit(a, w_stack, layer, *, out_dtype, tm=1024, tn=1024, tk=2048):
    m, k = a.shape
    n = w_stack.shape[2]
    tm, tn, tk = _tile(m, tm), _tile(n, tn), _tile(k, tk)
    return pl.pallas_call(
        _mm_ksplit_kernel,
        grid=(m // tm, n // tn, k // tk),
        in_specs=[pl.BlockSpec((tm, tk), lambda i, j, kk: (i, kk)),
                  pl.BlockSpec((None, tk, tn), lambda i, j, kk: (layer, kk, j))],
        out_specs=pl.BlockSpec((tm, tn), lambda i, j, kk: (i, j)),
        out_shape=jax.ShapeDtypeStruct((m, n), out_dtype),
        scratch_shapes=[pltpu.VMEM((tm, tn), F32)],
        compiler_params=_params(3),
        name="matmul_ksplit",
    )(a, w_stack)


def _ln_kernel(h_ref, mix_ref, g_ref, b_ref, o_ref, o16_ref):
    xf = DEEPNORM_ALPHA * h_ref[...] + mix_ref[...].astype(F32)
    mu = jnp.mean(xf, axis=-1, keepdims=True)
    xc = xf - mu
    var = jnp.mean(xc * xc, axis=-1, keepdims=True)
    y = xc * lax.rsqrt(var + LN_EPS) * g_ref[...] + b_ref[...]
    o_ref[...] = y
    o16_ref[...] = y.astype(BF16)


def _residual_ln(h, mix, g, b, *, tm=256):
    m, d = h.shape
    tm = _tile(m, tm, 8)
    row = pl.BlockSpec((tm, d), lambda i: (i, 0))
    vec = pl.BlockSpec((1, d), lambda i: (0, 0))
    return pl.pallas_call(
        _ln_kernel,
        grid=(m // tm,),
        in_specs=[row, row, vec, vec],
        out_specs=[row, row],
        out_shape=[jax.ShapeDtypeStruct((m, d), F32), jax.ShapeDtypeStruct((m, d), BF16)],
        compiler_params=_params(1),
        name="residual_layernorm",
    )(h, mix, g.reshape(1, d), b.reshape(1, d))


SB_TQ = 2048
SB_TK = 128
SB_GROUP = 2 * SB_TK
SB_HEADS_PER_STEP = 2
LOG2_E = 1.4426950408889634


def _sb_group(qs, k_ref, v_ref, sum_mat, key0, acc_ref, r_ref, masks, row0=0):
    dh = V7X_LANES
    for hb, q in enumerate(qs):
        cols = slice(hb * dh, (hb + 1) * dh)
        kb = k_ref[pl.ds(key0, SB_GROUP), cols]
        vb = v_ref[pl.ds(key0, SB_GROUP), cols]
        z = lax.dot_general(q[row0:], kb, (((1,), (1,)), ((), ())), preferred_element_type=F32)
        neg_abs = lax.bitcast_convert_type(
            lax.bitcast_convert_type(z, jnp.uint32) | jnp.uint32(0x80000000), F32)
        sp = jnp.maximum(z, 0.0) + jnp.log2(1.0 + jnp.exp2(neg_abs))
        log_beta = z - sp
        if masks is not None:
            sp = jnp.where(masks, sp, 0.0)
            log_beta = jnp.where(masks, log_beta, -jnp.inf)
        suffix = jnp.dot(sp.astype(BF16), sum_mat, preferred_element_type=F32)
        total = jnp.sum(sp, axis=1, keepdims=True)
        r = r_ref[row0:, cols]
        w = jnp.exp2(log_beta - suffix - jnp.concatenate([r, r], axis=1)).astype(BF16)
        acc_ref[row0:, cols] += jnp.dot(w, vb, preferred_element_type=F32)
        r_ref[row0:, cols] = r + total


def _sb_kernel(q_ref, k_ref, v_ref, sum_ref, o_ref, acc_ref, r_ref, *, scale, hps):
    i = pl.program_id(1)
    dh = V7X_LANES
    qs = [(q_ref[:, hb * dh:(hb + 1) * dh].astype(F32) * (scale * LOG2_E)).astype(BF16) for hb in range(hps)]
    sum_mat = sum_ref[...]
    acc_ref[...] = jnp.zeros_like(acc_ref)
    r_ref[...] = jnp.zeros_like(r_ref)
    q0 = pl.multiple_of(i * SB_TQ, SB_TQ)

    for d in reversed(range(SB_TQ // SB_GROUP)):
        row0 = d * SB_GROUP
        row = lax.broadcasted_iota(jnp.int32, (SB_TQ - row0, SB_GROUP), 0)
        col = lax.broadcasted_iota(jnp.int32, (SB_TQ - row0, SB_GROUP), 1)
        _sb_group(qs, k_ref, v_ref, sum_mat, q0 + row0, acc_ref, r_ref, col < row, row0)

    def body(it, carry):
        key0 = pl.multiple_of(q0 - (it + 1) * SB_GROUP, SB_GROUP)
        _sb_group(qs, k_ref, v_ref, sum_mat, key0, acc_ref, r_ref, None)
        return carry

    lax.fori_loop(0, i * (SB_TQ // SB_GROUP), body, 0)
    o_ref[...] = acc_ref[...].astype(o_ref.dtype)


def _sb_sum_matrix():
    j = lax.broadcasted_iota(jnp.int32, (SB_GROUP, SB_GROUP), 0)
    s = lax.broadcasted_iota(jnp.int32, (SB_GROUP, SB_GROUP), 1)
    return (j > s).astype(BF16)


def _sb_attention(qkv, *, heads):
    seq, d3 = qkv.shape
    d = d3 // 3
    dh = d // heads
    hps = min(SB_HEADS_PER_STEP, heads)
    assert dh == V7X_LANES and SB_TQ % SB_GROUP == 0 and seq % SB_TQ == 0 and heads % hps == 0
    hg = heads // hps
    w = hps * dh
    return pl.pallas_call(
        functools.partial(_sb_kernel, scale=1.0 / math.sqrt(dh), hps=hps),
        grid=(hg, seq // SB_TQ),
        in_specs=[pl.BlockSpec((SB_TQ, w), lambda g, i: (i, g)),
                  pl.BlockSpec((seq, w), lambda g, i: (0, hg + g)),
                  pl.BlockSpec((seq, w), lambda g, i: (0, 2 * hg + g)),
                  pl.BlockSpec((2 * SB_TK, 2 * SB_TK), lambda g, i: (0, 0))],
        out_specs=pl.BlockSpec((SB_TQ, w), lambda g, i: (i, g)),
        out_shape=jax.ShapeDtypeStruct((seq, d), BF16),
        scratch_shapes=[pltpu.VMEM((SB_TQ, w), F32),
                        pltpu.VMEM((SB_TQ, w), F32)],
        compiler_params=_params(2),
        name="stickbreak_attention",
    )(qkv, qkv, qkv, _sb_sum_matrix())


def _gates_kernel(x_ref, w_ref, b_ref, o_ref):
    x = x_ref[...]
    w = w_ref[...]
    xh = x.astype(BF16)
    xl = (x - xh.astype(F32)).astype(BF16)
    wh = w.astype(BF16)
    wl = (w - wh.astype(F32)).astype(BF16)
    acc = jnp.dot(xh, wh, preferred_element_type=F32)
    acc += jnp.dot(xh, wl, preferred_element_type=F32)
    acc += jnp.dot(xl, wh, preferred_element_type=F32)
    o_ref[...] = acc + b_ref[...]


def _ml_gates(x, w_pad, b_pad, *, tm=512):
    m, d = x.shape
    n = w_pad.shape[1]
    tm = _tile(m, tm, 8)
    return pl.pallas_call(
        _gates_kernel,
        grid=(m // tm,),
        in_specs=[pl.BlockSpec((tm, d), lambda i: (i, 0)),
                  pl.BlockSpec((d, n), lambda i: (0, 0)),
                  pl.BlockSpec((1, n), lambda i: (0, 0))],
        out_specs=pl.BlockSpec((tm, n), lambda i: (i, 0)),
        out_shape=jax.ShapeDtypeStruct((m, n), F32),
        compiler_params=_params(1),
        name="mlstm_gates",
    )(x, w_pad, b_pad)


def _mlstm_kernel(q_ref, k_ref, v_ref, o_ref, g_ref, nw_ref, y_ref, c_ref, m_ref, *, heads, dqk, dv):
    L = ML_CHUNK
    dva = dv + V7X_LANES

    @pl.when(pl.program_id(0) == 0)
    def _():
        c_ref[...] = jnp.zeros_like(c_ref)
        m_ref[...] = jnp.zeros_like(m_ref)

    r = lax.broadcasted_iota(jnp.int32, (L, L), 0)
    c = lax.broadcasted_iota(jnp.int32, (L, L), 1)
    eye = r == c
    causal = c <= r
    ones_col = (lax.broadcasted_iota(jnp.int32, (L, V7X_LANES), 1) == 0).astype(BF16)
    gates = g_ref[...]

    for h in range(heads):
        i_col = gates[:, h:h + 1]
        f_pre = gates[:, heads + h:heads + h + 1]
        lf_col = jnp.minimum(f_pre, 0.0) - jnp.log(1.0 + jnp.exp(-jnp.abs(f_pre)))
        lf_row = jnp.sum(jnp.where(eye, lf_col, 0.0), axis=0, keepdims=True)
        i_row = jnp.sum(jnp.where(eye, i_col, 0.0), axis=0, keepdims=True)
        b_col = jnp.sum(jnp.where(causal, lf_row, 0.0), axis=1, keepdims=True)
        b_row = jnp.sum(jnp.where(r <= c, lf_col, 0.0), axis=0, keepdims=True)
        g = jnp.sum(lf_col, axis=0, keepdims=True)
        m_st = m_ref[h][0:1, 0:1]

        dmat = jnp.where(causal, b_col - b_row + i_row, -jnp.inf)
        inter_log = b_col + m_st
        m_q = jnp.maximum(inter_log, jnp.max(dmat, axis=1, keepdims=True))
        pexp = jnp.exp(dmat - m_q)
        inter_scale = jnp.exp(inter_log - m_q)

        qh = q_ref[:, h * dqk:(h + 1) * dqk] * (dqk ** -0.5)
        kh = k_ref[:, h * dqk:(h + 1) * dqk]
        vh = jnp.concatenate([v_ref[:, h * dv:(h + 1) * dv], ones_col], axis=1)
        qk = lax.dot_general(qh, kh, (((1,), (1,)), ((), ())), preferred_element_type=F32)
        p = (qk * pexp).astype(BF16)
        c_st = c_ref[h]
        tot = inter_scale * jnp.dot(qh, c_st.astype(BF16), preferred_element_type=F32)
        tot += jnp.dot(p, vh, preferred_element_type=F32)
        num = tot[:, :dv]
        den = tot[:, dv:dv + 1]
        hh = num / jnp.maximum(jnp.abs(den), jnp.exp(-m_q))

        key_log = g - b_col + i_col
        m_new = jnp.maximum(g + m_st, jnp.max(key_log, axis=0, keepdims=True))
        kw = jnp.exp(key_log - m_new)
        decay = jnp.exp(g + m_st - m_new)
        kwk = (kw * kh.astype(F32)).astype(BF16)
        upd = lax.dot_general(kwk, vh, (((0,), (0,)), ((), ())), preferred_element_type=F32)
        c_ref[h] = decay * c_st + upd
        m_ref[h] = jnp.broadcast_to(m_new, m_ref.shape[1:])

        hn = hh * lax.rsqrt(jnp.mean(hh * hh, axis=-1, keepdims=True) + HEAD_NORM_EPS)
        hn = hn * nw_ref[:, h * dv:(h + 1) * dv]
        og = o_ref[:, h * dv:(h + 1) * dv].astype(F32)
        y_ref[:, h * dv:(h + 1) * dv] = (hn / (1.0 + jnp.exp(-og))).astype(y_ref.dtype)


def _mlstm(qkvo, gates, norm_w, *, heads):
    seq, cols = qkvo.shape
    d = norm_w.shape[-1]
    dv = d // heads
    dqk = (cols - 2 * d) // (2 * heads)
    qk_w = heads * dqk
    assert (2 * qk_w) % d == 0 and seq % ML_CHUNK == 0
    L = ML_CHUNK
    return pl.pallas_call(
        functools.partial(_mlstm_kernel, heads=heads, dqk=dqk, dv=dv),
        grid=(seq // L,),
        in_specs=[pl.BlockSpec((L, qk_w), lambda t: (t, 0)),
                  pl.BlockSpec((L, qk_w), lambda t: (t, 1)),
                  pl.BlockSpec((L, d), lambda t: (t, (2 * qk_w) // d)),
                  pl.BlockSpec((L, d), lambda t: (t, (2 * qk_w) // d + 1)),
                  pl.BlockSpec((L, gates.shape[1]), lambda t: (t, 0)),
                  pl.BlockSpec((1, d), lambda t: (0, 0))],
        out_specs=pl.BlockSpec((L, d), lambda t: (t, 0)),
        out_shape=jax.ShapeDtypeStruct((seq, d), BF16),
        scratch_shapes=[pltpu.VMEM((heads, dqk, dv + V7X_LANES), F32),
                        pltpu.VMEM((heads, 8, V7X_LANES), F32)],
        compiler_params=_params(1),
        name="mlstm_chunkwise",
    )(qkvo, qkvo, qkvo, qkvo, gates, norm_w.reshape(1, d))


def _conv_kernel(b_ref, c_ref, u_ref, cp_ref, up_ref, w_ref, y_ref):
    i = pl.program_id(0)
    cu = c_ref[...].astype(F32) * u_ref[...].astype(F32)
    halo = cp_ref[...].astype(F32) * up_ref[...].astype(F32)
    halo = jnp.where(i > 0, halo, 0.0)
    hl = halo.shape[0]
    prev1 = halo[hl - 1:hl, :]
    prev2 = halo[hl - 2:hl - 1, :]
    row = lax.broadcasted_iota(jnp.int32, cu.shape, 0)
    s1 = jnp.where(row == 0, prev1, pltpu.roll(cu, 1, axis=0))
    s2 = jnp.where(row == 0, prev2, jnp.where(row == 1, prev1, pltpu.roll(cu, 2, axis=0)))
    w = w_ref[...]
    conv = w[0:1, :] * s2 + w[1:2, :] * s1 + w[2:3, :] * cu
    y_ref[...] = (b_ref[...].astype(F32) * conv).astype(y_ref.dtype)


def _short_conv(bcu, conv_w, *, tm=512, tn=1024):
    seq, d3 = bcu.shape
    d = d3 // 3
    tm, tn = _tile(seq, tm, V7X_BF16_SUBLANES), _tile(d, tn)
    nb = d // tn
    hl = V7X_BF16_SUBLANES
    rb = tm // hl

    def halo_map(off):
        return lambda i, j: (jnp.maximum(i * rb - 1, 0), off + j)

    return pl.pallas_call(
        _conv_kernel,
        grid=(seq // tm, nb),
        in_specs=[pl.BlockSpec((tm, tn), lambda i, j: (i, j)),
                  pl.BlockSpec((tm, tn), lambda i, j: (i, nb + j)),
                  pl.BlockSpec((tm, tn), lambda i, j: (i, 2 * nb + j)),
                  pl.BlockSpec((hl, tn), halo_map(nb)),
                  pl.BlockSpec((hl, tn), halo_map(2 * nb)),
                  pl.BlockSpec((CONV_WIDTH, tn), lambda i, j: (0, j))],
        out_specs=pl.BlockSpec((tm, tn), lambda i, j: (i, j)),
        out_shape=jax.ShapeDtypeStruct((seq, d), BF16),
        compiler_params=_params(2),
        name="gated_short_conv",
    )(bcu, bcu, bcu, bcu, bcu, conv_w)


def kernel(x, sb_w_qkv, sb_w_out, ml_w_in, ml_b_gates, ml_norm_w, ml_w_out,
           sc_w_in, sc_conv_w, sc_w_out, mlp_w1, mlp_w2, ln_g, ln_b):
    bsz, seq, d = x.shape
    outs = []
    x2 = x.reshape(bsz * seq, d)
    for bi in range(bsz):
        h = x2 if bsz == 1 else x2[bi * seq:(bi + 1) * seq]
        h16 = h.astype(BF16)
        for layer in range(DEPTH):
            kind = layer % N_MIXERS
            slot = layer // N_MIXERS
            if kind == 0:
                qkv = _matmul(h16, sb_w_qkv, slot, out_dtype=BF16)
                y = _sb_attention(qkv, heads=SB_HEADS)
                w_out = sb_w_out
            elif kind == 1:
                n_main = ml_w_in.shape[-1] - 2 * ML_HEADS
                qkvo = _matmul(h16, ml_w_in, slot, n=n_main, out_dtype=BF16)
                pad = V7X_LANES - 2 * ML_HEADS
                w_g = jnp.pad(ml_w_in[slot, :, n_main:], ((0, 0), (0, pad)))
                b_g = jnp.pad(ml_b_gates[slot].reshape(1, -1), ((0, 0), (0, pad)))
                gates = _ml_gates(h, w_g, b_g)
                y = _mlstm(qkvo, gates, ml_norm_w[slot], heads=ML_HEADS)
                w_out = ml_w_out
            else:
                bcu = _matmul(h16, sc_w_in, slot, out_dtype=BF16)
                y = _short_conv(bcu, sc_conv_w[slot])
                w_out = sc_w_out
            mix = _matmul(y, w_out, slot, out_dtype=BF16)
            h, h16 = _residual_ln(h, mix, ln_g[layer, 0], ln_b[layer, 0])
            mid = _matmul(h16, mlp_w1, layer, out_dtype=BF16, act="relu2")
            ff = _matmul_ksplit(mid, mlp_w2, layer, out_dtype=BF16)
            h, h16 = _residual_ln(h, ff, ln_g[layer, 1], ln_b[layer, 1])
        outs.append(h)
    return outs[0][None] if bsz == 1 else jnp.stack(outs, axis=0)
```

```python
import functools
import math

import jax
import jax.numpy as jnp
from jax import lax
from jax.experimental import pallas as pl
from jax.experimental.pallas import tpu as pltpu

DEPTH = 4
N_MIXERS = 3
SB_HEADS = 32
ML_HEADS = 8
ML_CHUNK = 64
CONV_WIDTH = 3
DEEPNORM_ALPHA = (2 * DEPTH) ** 0.25
LN_EPS = 1e-5
HEAD_NORM_EPS = 1e-6

V7X_LANES = 128
V7X_BF16_SUBLANES = 16
V7X_VMEM_LIMIT_BYTES = 56 * 1024 * 1024

F32 = jnp.float32
BF16 = jnp.bfloat16


def _tile(n, target, align=V7X_LANES):
    if n <= target:
        return n
    t = (target // align) * align
    while n % t:
        t -= align
    return t


def _params(n_axes, flags=None):
    return pltpu.CompilerParams(
        dimension_semantics=("arbitrary",) * n_axes,
        vmem_limit_bytes=V7X_VMEM_LIMIT_BYTES,
        flags=flags)


def _mm_kernel(a_ref, w_ref, o_ref, *, act):
    acc = jnp.dot(a_ref[...], w_ref[...].astype(BF16), preferred_element_type=F32)
    if act == "relu2":
        r = jnp.maximum(acc, 0.0)
        acc = r * r
    o_ref[...] = acc.astype(o_ref.dtype)


def _matmul(a, w_stack, layer, *, out_dtype, n=None, act=None, tm=2048, tn=512):
    m, k = a.shape
    n = w_stack.shape[2] if n is None else n
    tm, tn = _tile(m, tm), _tile(n, tn)
    return pl.pallas_call(
        functools.partial(_mm_kernel, act=act),
        grid=(m // tm, n // tn),
        in_specs=[pl.BlockSpec((tm, k), lambda i, j: (i, 0)),
                  pl.BlockSpec((None, k, tn), lambda i, j: (layer, 0, j))],
        out_specs=pl.BlockSpec((tm, tn), lambda i, j: (i, j)),
        out_shape=jax.ShapeDtypeStruct((m, n), out_dtype),
        compiler_params=_params(2),
        name="matmul_fullk",
    )(a, w_stack)


def _mm_ksplit_kernel(a_ref, w_ref, o_ref, acc_ref):
    kk = pl.program_id(2)

    @pl.when(kk == 0)
    def _():
        acc_ref[...] = jnp.zeros_like(acc_ref)

    acc_ref[...] += jnp.dot(a_ref[...], w_ref[...].astype(BF16), preferred_element_type=F32)

    @pl.when(kk == pl.num_programs(2) - 1)
    def _():
        o_ref[...] = acc_ref[...].astype(o_ref.dtype)


def _matmul_ksplit(a, w_stack, layer, *, out_dtype, tm=1024, tn=1024, tk=2048):
    m, k = a.shape
    n = w_stack.shape[2]
    tm, tn, tk = _tile(m, tm), _tile(n, tn), _tile(k, tk)
    return pl.pallas_call(
        _mm_ksplit_kernel,
        grid=(m // tm, n // tn, k // tk),
        in_specs=[pl.BlockSpec((tm, tk), lambda i, j, kk: (i, kk)),
                  pl.BlockSpec((None, tk, tn), lambda i, j, kk: (layer, kk, j))],
        out_specs=pl.BlockSpec((tm, tn), lambda i, j, kk: (i, j)),
        out_shape=jax.ShapeDtypeStruct((m, n), out_dtype),
        scratch_shapes=[pltpu.VMEM((tm, tn), F32)],
        compiler_params=_params(3),
        name="matmul_ksplit",
    )(a, w_stack)


def _ln_kernel(h_ref, mix_ref, g_ref, b_ref, o_ref, o16_ref):
    xf = DEEPNORM_ALPHA * h_ref[...] + mix_ref[...].astype(F32)
    mu = jnp.mean(xf, axis=-1, keepdims=True)
    xc = xf - mu
    var = jnp.mean(xc * xc, axis=-1, keepdims=True)
    y = xc * lax.rsqrt(var + LN_EPS) * g_ref[...] + b_ref[...]
    o_ref[...] = y
    o16_ref[...] = y.astype(BF16)


def _residual_ln(h, mix, g, b, *, tm=256):
    m, d = h.shape
    tm = _tile(m, tm, 8)
    row = pl.BlockSpec((tm, d), lambda i: (i, 0))
    vec = pl.BlockSpec((1, d), lambda i: (0, 0))
    return pl.pallas_call(
        _ln_kernel,
        grid=(m // tm,),
        in_specs=[row, row, vec, vec],
        out_specs=[row, row],
        out_shape=[jax.ShapeDtypeStruct((m, d), F32), jax.ShapeDtypeStruct((m, d), BF16)],
        compiler_params=_params(1),
        name="residual_layernorm",
    )(h, mix, g.reshape(1, d), b.reshape(1, d))


SB_TQ = 2048
SB_TK = 128
SB_GROUP = 2 * SB_TK
SB_UNROLL = 4
SB_HEADS_PER_STEP = 2
LOG2_E = 1.4426950408889634
SB_EXP2_CLAMP = 126.0


def _sb_group(qs, k_ref, v_ref, sum_mat, key0, acc_ref, r_ref, masks, row0=0):
    dh = V7X_LANES
    for hb, q in enumerate(qs):
        cols = slice(hb * dh, (hb + 1) * dh)
        kb = k_ref[pl.ds(key0, SB_GROUP), cols]
        vb = v_ref[pl.ds(key0, SB_GROUP), cols]
        z = lax.dot_general(q[row0:], kb, (((1,), (1,)), ((), ())), preferred_element_type=F32)
        sp = jnp.maximum(jnp.log2(1.0 + jnp.exp2(jnp.minimum(z, SB_EXP2_CLAMP))), z)
        log_beta = z - sp
        if masks is not None:
            sp = jnp.where(masks, sp, 0.0)
            log_beta = jnp.where(masks, log_beta, -jnp.inf)
        r = r_ref[row0:, cols]
        later = jnp.dot(sp.astype(BF16), sum_mat, preferred_element_type=F32) + jnp.concatenate([r, r], axis=1)
        total = jnp.sum(sp, axis=1, keepdims=True)
        w = jnp.exp2(log_beta - later).astype(BF16)
        acc_ref[row0:, cols] += jnp.dot(w, vb, preferred_element_type=F32)
        r_ref[row0:, cols] = r + total


def _sb_kernel(q_ref, k_ref, v_ref, sum_ref, o_ref, acc_ref, r_ref, *, scale, hps):
    i = pl.program_id(1)
    dh = V7X_LANES
    qs = [(q_ref[:, hb * dh:(hb + 1) * dh].astype(F32) * (scale * LOG2_E)).astype(BF16) for hb in range(hps)]
    sum_mat = sum_ref[...]
    acc_ref[...] = jnp.zeros_like(acc_ref)
    r_ref[...] = jnp.zeros_like(r_ref)
    q0 = pl.multiple_of(i * SB_TQ, SB_TQ)

    for d in reversed(range(SB_TQ // SB_GROUP)):
        row0 = d * SB_GROUP
        row = lax.broadcasted_iota(jnp.int32, (SB_TQ - row0, SB_GROUP), 0)
        col = lax.broadcasted_iota(jnp.int32, (SB_TQ - row0, SB_GROUP), 1)
        _sb_group(qs, k_ref, v_ref, sum_mat, q0 + row0, acc_ref, r_ref, col < row, row0)

    def body(it, carry):
        for u in range(SB_UNROLL):
            key0 = pl.multiple_of(q0 - (it * SB_UNROLL + u + 1) * SB_GROUP, SB_GROUP)
            _sb_group(qs, k_ref, v_ref, sum_mat, key0, acc_ref, r_ref, None)
        return carry

    lax.fori_loop(0, i * (SB_TQ // (SB_GROUP * SB_UNROLL)), body, 0)
    o_ref[...] = acc_ref[...].astype(o_ref.dtype)


def _sb_sum_matrix():
    j = lax.broadcasted_iota(jnp.int32, (SB_GROUP, SB_GROUP), 0)
    s = lax.broadcasted_iota(jnp.int32, (SB_GROUP, SB_GROUP), 1)
    return (j > s).astype(BF16)


def _sb_attention(qkv, *, heads):
    seq, d3 = qkv.shape
    d = d3 // 3
    dh = d // heads
    hps = min(SB_HEADS_PER_STEP, heads)
    assert dh == V7X_LANES and SB_TQ % SB_GROUP == 0 and seq % SB_TQ == 0 and heads % hps == 0
    hg = heads // hps
    w = hps * dh
    return pl.pallas_call(
        functools.partial(_sb_kernel, scale=1.0 / math.sqrt(dh), hps=hps),
        grid=(hg, seq // SB_TQ),
        in_specs=[pl.BlockSpec((SB_TQ, w), lambda g, i: (i, g)),
                  pl.BlockSpec((seq, w), lambda g, i: (0, hg + g)),
                  pl.BlockSpec((seq, w), lambda g, i: (0, 2 * hg + g)),
                  pl.BlockSpec((2 * SB_TK, 2 * SB_TK), lambda g, i: (0, 0))],
        out_specs=pl.BlockSpec((SB_TQ, w), lambda g, i: (i, g)),
        out_shape=jax.ShapeDtypeStruct((seq, d), BF16),
        scratch_shapes=[pltpu.VMEM((SB_TQ, w), F32),
                        pltpu.VMEM((SB_TQ, w), F32)],
        compiler_params=_params(2),
        name="stickbreak_attention",
    )(qkv, qkv, qkv, _sb_sum_matrix())


def _gates_kernel(x_ref, w_ref, b_ref, o_ref):
    x = x_ref[...]
    w = w_ref[...]
    xh = x.astype(BF16)
    xl = (x - xh.astype(F32)).astype(BF16)
    wh = w.astype(BF16)
    wl = (w - wh.astype(F32)).astype(BF16)
    acc = jnp.dot(xh, wh, preferred_element_type=F32)
    acc += jnp.dot(xh, wl, preferred_element_type=F32)
    acc += jnp.dot(xl, wh, preferred_element_type=F32)
    o_ref[...] = acc + b_ref[...]


def _ml_gates(x, w_pad, b_pad, *, tm=512):
    m, d = x.shape
    n = w_pad.shape[1]
    tm = _tile(m, tm, 8)
    return pl.pallas_call(
        _gates_kernel,
        grid=(m // tm,),
        in_specs=[pl.BlockSpec((tm, d), lambda i: (i, 0)),
                  pl.BlockSpec((d, n), lambda i: (0, 0)),
                  pl.BlockSpec((1, n), lambda i: (0, 0))],
        out_specs=pl.BlockSpec((tm, n), lambda i: (i, 0)),
        out_shape=jax.ShapeDtypeStruct((m, n), F32),
        compiler_params=_params(1),
        name="mlstm_gates",
    )(x, w_pad, b_pad)


def _mlstm_kernel(q_ref, k_ref, v_ref, o_ref, g_ref, nw_ref, y_ref, c_ref, m_ref, *, heads, dqk, dv):
    L = ML_CHUNK
    dva = dv + V7X_LANES

    @pl.when(pl.program_id(0) == 0)
    def _():
        c_ref[...] = jnp.zeros_like(c_ref)
        m_ref[...] = jnp.zeros_like(m_ref)

    r = lax.broadcasted_iota(jnp.int32, (L, L), 0)
    c = lax.broadcasted_iota(jnp.int32, (L, L), 1)
    eye = r == c
    causal = c <= r
    ones_col = (lax.broadcasted_iota(jnp.int32, (L, V7X_LANES), 1) == 0).astype(BF16)
    gates = g_ref[...]

    for h in range(heads):
        i_col = gates[:, h:h + 1]
        f_pre = gates[:, heads + h:heads + h + 1]
        lf_col = jnp.minimum(f_pre, 0.0) - jnp.log(1.0 + jnp.exp(-jnp.abs(f_pre)))
        lf_row = jnp.sum(jnp.where(eye, lf_col, 0.0), axis=0, keepdims=True)
        i_row = jnp.sum(jnp.where(eye, i_col, 0.0), axis=0, keepdims=True)
        b_col = jnp.sum(jnp.where(causal, lf_row, 0.0), axis=1, keepdims=True)
        b_row = jnp.sum(jnp.where(r <= c, lf_col, 0.0), axis=0, keepdims=True)
        g = jnp.sum(lf_col, axis=0, keepdims=True)
        m_st = m_ref[h][0:1, 0:1]

        dmat = jnp.where(causal, b_col - b_row + i_row, -jnp.inf)
        inter_log = b_col + m_st
        m_q = jnp.maximum(inter_log, jnp.max(dmat, axis=1, keepdims=True))
        pexp = jnp.exp(dmat - m_q)
        inter_scale = jnp.exp(inter_log - m_q)

        qh = q_ref[:, h * dqk:(h + 1) * dqk] * (dqk ** -0.5)
        kh = k_ref[:, h * dqk:(h + 1) * dqk]
        vh = jnp.concatenate([v_ref[:, h * dv:(h + 1) * dv], ones_col], axis=1)
        qk = lax.dot_general(qh, kh, (((1,), (1,)), ((), ())), preferred_element_type=F32)
        p = (qk * pexp).astype(BF16)
        c_st = c_ref[h]
        tot = inter_scale * jnp.dot(qh, c_st.astype(BF16), preferred_element_type=F32)
        tot += jnp.dot(p, vh, preferred_element_type=F32)
        num = tot[:, :dv]
        den = tot[:, dv:dv + 1]
        hh = num / jnp.maximum(jnp.abs(den), jnp.exp(-m_q))

        key_log = g - b_col + i_col
        m_new = jnp.maximum(g + m_st, jnp.max(key_log, axis=0, keepdims=True))
        kw = jnp.exp(key_log - m_new)
        decay = jnp.exp(g + m_st - m_new)
        kwk = (kw * kh.astype(F32)).astype(BF16)
        upd = lax.dot_general(kwk, vh, (((0,), (0,)), ((), ())), preferred_element_type=F32)
        c_ref[h] = decay * c_st + upd
        m_ref[h] = jnp.broadcast_to(m_new, m_ref.shape[1:])

        hn = hh * lax.rsqrt(jnp.mean(hh * hh, axis=-1, keepdims=True) + HEAD_NORM_EPS)
        hn = hn * nw_ref[:, h * dv:(h + 1) * dv]
        og = o_ref[:, h * dv:(h + 1) * dv].astype(F32)
        y_ref[:, h * dv:(h + 1) * dv] = (hn / (1.0 + jnp.exp(-og))).astype(y_ref.dtype)


def _mlstm(qkvo, gates, norm_w, *, heads):
    seq, cols = qkvo.shape
    d = norm_w.shape[-1]
    dv = d // heads
    dqk = (cols - 2 * d) // (2 * heads)
    qk_w = heads * dqk
    assert (2 * qk_w) % d == 0 and seq % ML_CHUNK == 0
    L = ML_CHUNK
    return pl.pallas_call(
        functools.partial(_mlstm_kernel, heads=heads, dqk=dqk, dv=dv),
        grid=(seq // L,),
        in_specs=[pl.BlockSpec((L, qk_w), lambda t: (t, 0)),
                  pl.BlockSpec((L, qk_w), lambda t: (t, 1)),
                  pl.BlockSpec((L, d), lambda t: (t, (2 * qk_w) // d)),
                  pl.BlockSpec((L, d), lambda t: (t, (2 * qk_w) // d + 1)),
                  pl.BlockSpec((L, gates.shape[1]), lambda t: (t, 0)),
                  pl.BlockSpec((1, d), lambda t: (0, 0))],
        out_specs=pl.BlockSpec((L, d), lambda t: (t, 0)),
        out_shape=jax.ShapeDtypeStruct((seq, d), BF16),
        scratch_shapes=[pltpu.VMEM((heads, dqk, dv + V7X_LANES), F32),
                        pltpu.VMEM((heads, 8, V7X_LANES), F32)],
        compiler_params=_params(1),
        name="mlstm_chunkwise",
    )(qkvo, qkvo, qkvo, qkvo, gates, norm_w.reshape(1, d))


def _conv_kernel(b_ref, c_ref, u_ref, cp_ref, up_ref, w_ref, y_ref):
    i = pl.program_id(0)
    cu = c_ref[...].astype(F32) * u_ref[...].astype(F32)
    halo = cp_ref[...].astype(F32) * up_ref[...].astype(F32)
    halo = jnp.where(i > 0, halo, 0.0)
    hl = halo.shape[0]
    prev1 = halo[hl - 1:hl, :]
    prev2 = halo[hl - 2:hl - 1, :]
    row = lax.broadcasted_iota(jnp.int32, cu.shape, 0)
    s1 = jnp.where(row == 0, prev1, pltpu.roll(cu, 1, axis=0))
    s2 = jnp.where(row == 0, prev2, jnp.where(row == 1, prev1, pltpu.roll(cu, 2, axis=0)))
    w = w_ref[...]
    conv = w[0:1, :] * s2 + w[1:2, :] * s1 + w[2:3, :] * cu
    y_ref[...] = (b_ref[...].astype(F32) * conv).astype(y_ref.dtype)


def _short_conv(bcu, conv_w, *, tm=1024, tn=1024):
    seq, d3 = bcu.shape
    d = d3 // 3
    tm, tn = _tile(seq, tm, V7X_BF16_SUBLANES), _tile(d, tn)
    nb = d // tn
    hl = V7X_BF16_SUBLANES
    rb = tm // hl

    def halo_map(off):
        return lambda i, j: (jnp.maximum(i * rb - 1, 0), off + j)

    return pl.pallas_call(
        _conv_kernel,
        grid=(seq // tm, nb),
        in_specs=[pl.BlockSpec((tm, tn), lambda i, j: (i, j)),
                  pl.BlockSpec((tm, tn), lambda i, j: (i, nb + j)),
                  pl.BlockSpec((tm, tn), lambda i, j: (i, 2 * nb + j)),
                  pl.BlockSpec((hl, tn), halo_map(nb)),
                  pl.BlockSpec((hl, tn), halo_map(2 * nb)),
                  pl.BlockSpec((CONV_WIDTH, tn), lambda i, j: (0, j))],
        out_specs=pl.BlockSpec((tm, tn), lambda i, j: (i, j)),
        out_shape=jax.ShapeDtypeStruct((seq, d), BF16),
        compiler_params=_params(2),
        name="gated_short_conv",
    )(bcu, bcu, bcu, bcu, bcu, conv_w)


def kernel(x, sb_w_qkv, sb_w_out, ml_w_in, ml_b_gates, ml_norm_w, ml_w_out,
           sc_w_in, sc_conv_w, sc_w_out, mlp_w1, mlp_w2, ln_g, ln_b):
    bsz, seq, d = x.shape
    outs = []
    x2 = x.reshape(bsz * seq, d)
    for bi in range(bsz):
        h = x2 if bsz == 1 else x2[bi * seq:(bi + 1) * seq]
        h16 = h.astype(BF16)
        for layer in range(DEPTH):
            kind = layer % N_MIXERS
            slot = layer // N_MIXERS
            if kind == 0:
                qkv = _matmul(h16, sb_w_qkv, slot, out_dtype=BF16)
                y = _sb_attention(qkv, heads=SB_HEADS)
                w_out = sb_w_out
            elif kind == 1:
                n_main = ml_w_in.shape[-1] - 2 * ML_HEADS
                qkvo = _matmul(h16, ml_w_in, slot, n=n_main, out_dtype=BF16)
                pad = V7X_LANES - 2 * ML_HEADS
                w_g = jnp.pad(ml_w_in[slot, :, n_main:], ((0, 0), (0, pad)))
                b_g = jnp.pad(ml_b_gates[slot].reshape(1, -1), ((0, 0), (0, pad)))
                gates = _ml_gates(h, w_g, b_g)
                y = _mlstm(qkvo, gates, ml_norm_w[slot], heads=ML_HEADS)
                w_out = ml_w_out
            else:
                bcu = _matmul(h16, sc_w_in, slot, out_dtype=BF16)
                y = _short_conv(bcu, sc_conv_w[slot])
                w_out = sc_w_out
            mix = _matmul(y, w_out, slot, out_dtype=BF16)
            h, h16 = _residual_ln(h, mix, ln_g[layer, 0], ln_b[layer, 0])
            mid = _matmul(h16, mlp_w1, layer, out_dtype=BF16, act="relu2")
            ff = _matmul_ksplit(mid, mlp_w2, layer, out_dtype=BF16)
            h, h16 = _residual_ln(h, ff, ln_g[layer, 1], ln_b[layer, 1])
        outs.append(h)
    return outs[0][None] if bsz == 1 else jnp.stack(outs, axis=0)
```

```python
import functools
import math

import jax
import jax.numpy as jnp
from jax import lax
from jax.experimental import pallas as pl
from jax.experimental.pallas import tpu as pltpu

DEPTH = 4
N_MIXERS = 3
SB_HEADS = 32
ML_HEADS = 8
ML_CHUNK = 64
CONV_WIDTH = 3
DEEPNORM_ALPHA = (2 * DEPTH) ** 0.25
LN_EPS = 1e-5
HEAD_NORM_EPS = 1e-6

V7X_LANES = 128
V7X_BF16_SUBLANES = 16
V7X_VMEM_LIMIT_BYTES = 56 * 1024 * 1024

F32 = jnp.float32
BF16 = jnp.bfloat16


def _tile(n, target, align=V7X_LANES):
    if n <= target:
        return n
    t = (target // align) * align
    while n % t:
        t -= align
    return t


def _params(n_axes):
    return pltpu.CompilerParams(
        dimension_semantics=("arbitrary",) * n_axes,
        vmem_limit_bytes=V7X_VMEM_LIMIT_BYTES)


def _mm_kernel(a_ref, w_ref, o_ref, *, act):
    acc = jnp.dot(a_ref[...], w_ref[...].astype(BF16), preferred_element_type=F32)
    if act == "relu2":
        r = jnp.maximum(acc, 0.0)
        acc = r * r
    o_ref[...] = acc.astype(o_ref.dtype)


def _matmul(a, w_stack, layer, *, out_dtype, n=None, act=None, tm=2048, tn=512):
    m, k = a.shape
    n = w_stack.shape[2] if n is None else n
    tm, tn = _tile(m, tm), _tile(n, tn)
    return pl.pallas_call(
        functools.partial(_mm_kernel, act=act),
        grid=(m // tm, n // tn),
        in_specs=[pl.BlockSpec((tm, k), lambda i, j: (i, 0)),
                  pl.BlockSpec((None, k, tn), lambda i, j: (layer, 0, j))],
        out_specs=pl.BlockSpec((tm, tn), lambda i, j: (i, j)),
        out_shape=jax.ShapeDtypeStruct((m, n), out_dtype),
        compiler_params=_params(2),
        name="matmul_fullk",
    )(a, w_stack)


def _mm_ksplit_kernel(a_ref, w_ref, o_ref, acc_ref):
    kk = pl.program_id(2)

    @pl.when(kk == 0)
    def _():
        acc_ref[...] = jnp.zeros_like(acc_ref)

    acc_ref[...] += jnp.dot(a_ref[...], w_ref[...].astype(BF16), preferred_element_type=F32)

    @pl.when(kk == pl.num_programs(2) - 1)
    def _():
        o_ref[...] = acc_ref[...].astype(o_ref.dtype)


def _matmul_ksplit(a, w_stack, layer, *, out_dtype, tm=1024, tn=1024, tk=2048):
    m, k = a.shape
    n = w_stack.shape[2]
    tm, tn, tk = _tile(m, tm), _tile(n, tn), _tile(k, tk)
    return pl.pallas_call(
        _mm_ksplit_kernel,
        grid=(m // tm, n // tn, k // tk),
        in_specs=[pl.BlockSpec((tm, tk), lambda i, j, kk: (i, kk)),
                  pl.BlockSpec((None, tk, tn), lambda i, j, kk: (layer, kk, j))],
        out_specs=pl.BlockSpec((tm, tn), lambda i, j, kk: (i, j)),
        out_shape=jax.ShapeDtypeStruct((m, n), out_dtype),
        scratch_shapes=[pltpu.VMEM((tm, tn), F32)],
        compiler_params=_params(3),
        name="matmul_ksplit",
    )(a, w_stack)


def _ln_kernel(h_ref, mix_ref, g_ref, b_ref, o_ref, o16_ref):
    xf = DEEPNORM_ALPHA * h_ref[...] + mix_ref[...].astype(F32)
    mu = jnp.mean(xf, axis=-1, keepdims=True)
    xc = xf - mu
    var = jnp.mean(xc * xc, axis=-1, keepdims=True)
    y = xc * lax.rsqrt(var + LN_EPS) * g_ref[...] + b_ref[...]
    o_ref[...] = y
    o16_ref[...] = y.astype(BF16)


def _residual_ln(h, mix, g, b, *, tm=256):
    m, d = h.shape
    tm = _tile(m, tm, 8)
    row = pl.BlockSpec((tm, d), lambda i: (i, 0))
    vec = pl.BlockSpec((1, d), lambda i: (0, 0))
    return pl.pallas_call(
        _ln_kernel,
        grid=(m // tm,),
        in_specs=[row, row, vec, vec],
        out_specs=[row, row],
        out_shape=[jax.ShapeDtypeStruct((m, d), F32), jax.ShapeDtypeStruct((m, d), BF16)],
        compiler_params=_params(1),
        name="residual_layernorm",
    )(h, mix, g.reshape(1, d), b.reshape(1, d))


SB_TQ = 2048
SB_TK = 128
SB_GROUP = 2 * SB_TK
SB_UNROLL = 4
SB_HEADS_PER_STEP = 2
LOG2_E = 1.4426950408889634
SB_EXP2_CLAMP = 126.0


def _sb_group(qs, k_ref, v_ref, sum_mat, key0, acc_ref, r_ref, masks, row0=0):
    dh = V7X_LANES
    for hb, q in enumerate(qs):
        cols = slice(hb * dh, (hb + 1) * dh)
        kb = k_ref[pl.ds(key0, SB_GROUP), cols]
        vb = v_ref[pl.ds(key0, SB_GROUP), cols]
        z = lax.dot_general(q[row0:], kb, (((1,), (1,)), ((), ())), preferred_element_type=F32)
        sp = jnp.maximum(jnp.log2(1.0 + jnp.exp2(jnp.minimum(z, SB_EXP2_CLAMP))), z)
        log_beta = z - sp
        if masks is not None:
            sp = jnp.where(masks, sp, 0.0)
            log_beta = jnp.where(masks, log_beta, -jnp.inf)
        r = r_ref[row0:, cols]
        later = jnp.dot(sp.astype(BF16), sum_mat, preferred_element_type=F32) + jnp.concatenate([r, r], axis=1)
        total = jnp.sum(sp, axis=1, keepdims=True)
        w = jnp.exp2(log_beta - later).astype(BF16)
        acc_ref[row0:, cols] += jnp.dot(w, vb, preferred_element_type=F32)
        r_ref[row0:, cols] = r + total


def _sb_kernel(q_ref, k_ref, v_ref, sum_ref, o_ref, acc_ref, r_ref, *, scale, hps):
    i = pl.program_id(1)
    dh = V7X_LANES
    qs = [(q_ref[:, hb * dh:(hb + 1) * dh].astype(F32) * (scale * LOG2_E)).astype(BF16) for hb in range(hps)]
    sum_mat = sum_ref[...]
    acc_ref[...] = jnp.zeros_like(acc_ref)
    r_ref[...] = jnp.zeros_like(r_ref)
    q0 = pl.multiple_of(i * SB_TQ, SB_TQ)

    for d in reversed(range(SB_TQ // SB_GROUP)):
        row0 = d * SB_GROUP
        row = lax.broadcasted_iota(jnp.int32, (SB_TQ - row0, SB_GROUP), 0)
        col = lax.broadcasted_iota(jnp.int32, (SB_TQ - row0, SB_GROUP), 1)
        _sb_group(qs, k_ref, v_ref, sum_mat, q0 + row0, acc_ref, r_ref, col < row, row0)

    def body(it, carry):
        for u in range(SB_UNROLL):
            key0 = pl.multiple_of(q0 - (it * SB_UNROLL + u + 1) * SB_GROUP, SB_GROUP)
            _sb_group(qs, k_ref, v_ref, sum_mat, key0, acc_ref, r_ref, None)
        return carry

    lax.fori_loop(0, i * (SB_TQ // (SB_GROUP * SB_UNROLL)), body, 0)
    o_ref[...] = acc_ref[...].astype(o_ref.dtype)


def _sb_sum_matrix():
    j = lax.broadcasted_iota(jnp.int32, (SB_GROUP, SB_GROUP), 0)
    s = lax.broadcasted_iota(jnp.int32, (SB_GROUP, SB_GROUP), 1)
    return (j > s).astype(BF16)


def _sb_attention(qkv, *, heads):
    seq, d3 = qkv.shape
    d = d3 // 3
    dh = d // heads
    hps = min(SB_HEADS_PER_STEP, heads)
    assert dh == V7X_LANES and SB_TQ % (SB_GROUP * SB_UNROLL) == 0 and seq % SB_TQ == 0 and heads % hps == 0
    hg = heads // hps
    w = hps * dh
    return pl.pallas_call(
        functools.partial(_sb_kernel, scale=1.0 / math.sqrt(dh), hps=hps),
        grid=(hg, seq // SB_TQ),
        in_specs=[pl.BlockSpec((SB_TQ, w), lambda g, i: (i, g)),
                  pl.BlockSpec((seq, w), lambda g, i: (0, hg + g)),
                  pl.BlockSpec((seq, w), lambda g, i: (0, 2 * hg + g)),
                  pl.BlockSpec((SB_GROUP, SB_GROUP), lambda g, i: (0, 0))],
        out_specs=pl.BlockSpec((SB_TQ, w), lambda g, i: (i, g)),
        out_shape=jax.ShapeDtypeStruct((seq, d), BF16),
        scratch_shapes=[pltpu.VMEM((SB_TQ, w), F32),
                        pltpu.VMEM((SB_TQ, w), F32)],
        compiler_params=_params(2),
        name="stickbreak_attention",
    )(qkv, qkv, qkv, _sb_sum_matrix())


def _gates_kernel(x_ref, w_ref, b_ref, o_ref):
    x = x_ref[...]
    w = w_ref[...]
    xh = x.astype(BF16)
    xl = (x - xh.astype(F32)).astype(BF16)
    wh = w.astype(BF16)
    wl = (w - wh.astype(F32)).astype(BF16)
    acc = jnp.dot(xh, wh, preferred_element_type=F32)
    acc += jnp.dot(xh, wl, preferred_element_type=F32)
    acc += jnp.dot(xl, wh, preferred_element_type=F32)
    o_ref[...] = acc + b_ref[...]


def _ml_gates(x, w_pad, b_pad, *, tm=512):
    m, d = x.shape
    n = w_pad.shape[1]
    tm = _tile(m, tm, 8)
    return pl.pallas_call(
        _gates_kernel,
        grid=(m // tm,),
        in_specs=[pl.BlockSpec((tm, d), lambda i: (i, 0)),
                  pl.BlockSpec((d, n), lambda i: (0, 0)),
                  pl.BlockSpec((1, n), lambda i: (0, 0))],
        out_specs=pl.BlockSpec((tm, n), lambda i: (i, 0)),
        out_shape=jax.ShapeDtypeStruct((m, n), F32),
        compiler_params=_params(1),
        name="mlstm_gates",
    )(x, w_pad, b_pad)


def _mlstm_kernel(q_ref, k_ref, v_ref, o_ref, g_ref, nw_ref, y_ref, c_ref, m_ref, *, heads, dqk, dv):
    L = ML_CHUNK
    dva = dv + V7X_LANES

    @pl.when(pl.program_id(0) == 0)
    def _():
        c_ref[...] = jnp.zeros_like(c_ref)
        m_ref[...] = jnp.zeros_like(m_ref)

    r = lax.broadcasted_iota(jnp.int32, (L, L), 0)
    c = lax.broadcasted_iota(jnp.int32, (L, L), 1)
    eye = r == c
    causal = c <= r
    ones_col = (lax.broadcasted_iota(jnp.int32, (L, V7X_LANES), 1) == 0).astype(BF16)
    gates = g_ref[...]

    for h in range(heads):
        i_col = gates[:, h:h + 1]
        f_pre = gates[:, heads + h:heads + h + 1]
        lf_col = jnp.minimum(f_pre, 0.0) - jnp.log(1.0 + jnp.exp(-jnp.abs(f_pre)))
        lf_row = jnp.sum(jnp.where(eye, lf_col, 0.0), axis=0, keepdims=True)
        i_row = jnp.sum(jnp.where(eye, i_col, 0.0), axis=0, keepdims=True)
        b_col = jnp.sum(jnp.where(causal, lf_row, 0.0), axis=1, keepdims=True)
        b_row = jnp.sum(jnp.where(r <= c, lf_col, 0.0), axis=0, keepdims=True)
        g = jnp.sum(lf_col, axis=0, keepdims=True)
        m_st = m_ref[h][0:1, 0:1]

        dmat = jnp.where(causal, b_col - b_row + i_row, -jnp.inf)
        inter_log = b_col + m_st
        m_q = jnp.maximum(inter_log, jnp.max(dmat, axis=1, keepdims=True))
        pexp = jnp.exp(dmat - m_q)
        inter_scale = jnp.exp(inter_log - m_q)

        qh = q_ref[:, h * dqk:(h + 1) * dqk] * (dqk ** -0.5)
        kh = k_ref[:, h * dqk:(h + 1) * dqk]
        vh = jnp.concatenate([v_ref[:, h * dv:(h + 1) * dv], ones_col], axis=1)
        qk = lax.dot_general(qh, kh, (((1,), (1,)), ((), ())), preferred_element_type=F32)
        p = (qk * pexp).astype(BF16)
        c_st = c_ref[h]
        tot = inter_scale * jnp.dot(qh, c_st.astype(BF16), preferred_element_type=F32)
        tot += jnp.dot(p, vh, preferred_element_type=F32)
        num = tot[:, :dv]
        den = tot[:, dv:dv + 1]
        hh = num / jnp.maximum(jnp.abs(den), jnp.exp(-m_q))

        key_log = g - b_col + i_col
        m_new = jnp.maximum(g + m_st, jnp.max(key_log, axis=0, keepdims=True))
        kw = jnp.exp(key_log - m_new)
        decay = jnp.exp(g + m_st - m_new)
        kwk = (kw * kh.astype(F32)).astype(BF16)
        upd = lax.dot_general(kwk, vh, (((0,), (0,)), ((), ())), preferred_element_type=F32)
        c_ref[h] = decay * c_st + upd
        m_ref[h] = jnp.broadcast_to(m_new, m_ref.shape[1:])

        hn = hh * lax.rsqrt(jnp.mean(hh * hh, axis=-1, keepdims=True) + HEAD_NORM_EPS)
        hn = hn * nw_ref[:, h * dv:(h + 1) * dv]
        og = o_ref[:, h * dv:(h + 1) * dv].astype(F32)
        y_ref[:, h * dv:(h + 1) * dv] = (hn / (1.0 + jnp.exp(-og))).astype(y_ref.dtype)


def _mlstm(qkvo, gates, norm_w, *, heads):
    seq, cols = qkvo.shape
    d = norm_w.shape[-1]
    dv = d // heads
    dqk = (cols - 2 * d) // (2 * heads)
    qk_w = heads * dqk
    assert (2 * qk_w) % d == 0 and seq % ML_CHUNK == 0
    L = ML_CHUNK
    return pl.pallas_call(
        functools.partial(_mlstm_kernel, heads=heads, dqk=dqk, dv=dv),
        grid=(seq // L,),
        in_specs=[pl.BlockSpec((L, qk_w), lambda t: (t, 0)),
                  pl.BlockSpec((L, qk_w), lambda t: (t, 1)),
                  pl.BlockSpec((L, d), lambda t: (t, (2 * qk_w) // d)),
                  pl.BlockSpec((L, d), lambda t: (t, (2 * qk_w) // d + 1)),
                  pl.BlockSpec((L, gates.shape[1]), lambda t: (t, 0)),
                  pl.BlockSpec((1, d), lambda t: (0, 0))],
        out_specs=pl.BlockSpec((L, d), lambda t: (t, 0)),
        out_shape=jax.ShapeDtypeStruct((seq, d), BF16),
        scratch_shapes=[pltpu.VMEM((heads, dqk, dv + V7X_LANES), F32),
                        pltpu.VMEM((heads, 8, V7X_LANES), F32)],
        compiler_params=_params(1),
        name="mlstm_chunkwise",
    )(qkvo, qkvo, qkvo, qkvo, gates, norm_w.reshape(1, d))


def _conv_kernel(b_ref, c_ref, u_ref, cp_ref, up_ref, w_ref, y_ref):
    i = pl.program_id(0)
    cu = c_ref[...].astype(F32) * u_ref[...].astype(F32)
    halo = cp_ref[...].astype(F32) * up_ref[...].astype(F32)
    halo = jnp.where(i > 0, halo, 0.0)
    hl = halo.shape[0]
    prev1 = halo[hl - 1:hl, :]
    prev2 = halo[hl - 2:hl - 1, :]
    row = lax.broadcasted_iota(jnp.int32, cu.shape, 0)
    s1 = jnp.where(row == 0, prev1, pltpu.roll(cu, 1, axis=0))
    s2 = jnp.where(row == 0, prev2, jnp.where(row == 1, prev1, pltpu.roll(cu, 2, axis=0)))
    w = w_ref[...]
    conv = w[0:1, :] * s2 + w[1:2, :] * s1 + w[2:3, :] * cu
    y_ref[...] = (b_ref[...].astype(F32) * conv).astype(y_ref.dtype)


def _short_conv(bcu, conv_w, *, tm=1024, tn=1024):
    seq, d3 = bcu.shape
    d = d3 // 3
    tm, tn = _tile(seq, tm, V7X_BF16_SUBLANES), _tile(d, tn)
    nb = d // tn
    hl = V7X_BF16_SUBLANES
    rb = tm // hl

    def halo_map(off):
        return lambda i, j: (jnp.maximum(i * rb - 1, 0), off + j)

    return pl.pallas_call(
        _conv_kernel,
        grid=(seq // tm, nb),
        in_specs=[pl.BlockSpec((tm, tn), lambda i, j: (i, j)),
                  pl.BlockSpec((tm, tn), lambda i, j: (i, nb + j)),
                  pl.BlockSpec((tm, tn), lambda i, j: (i, 2 * nb + j)),
                  pl.BlockSpec((hl, tn), halo_map(nb)),
                  pl.BlockSpec((hl, tn), halo_map(2 * nb)),
                  pl.BlockSpec((CONV_WIDTH, tn), lambda i, j: (0, j))],
        out_specs=pl.BlockSpec((tm, tn), lambda i, j: (i, j)),
        out_shape=jax.ShapeDtypeStruct((seq, d), BF16),
        compiler_params=_params(2),
        name="gated_short_conv",
    )(bcu, bcu, bcu, bcu, bcu, conv_w)


def kernel(x, sb_w_qkv, sb_w_out, ml_w_in, ml_b_gates, ml_norm_w, ml_w_out,
           sc_w_in, sc_conv_w, sc_w_out, mlp_w1, mlp_w2, ln_g, ln_b):
    bsz, seq, d = x.shape
    outs = []
    x2 = x.reshape(bsz * seq, d)
    for bi in range(bsz):
        h = x2 if bsz == 1 else x2[bi * seq:(bi + 1) * seq]
        h16 = h.astype(BF16)
        for layer in range(DEPTH):
            kind = layer % N_MIXERS
            slot = layer // N_MIXERS
            if kind == 0:
                qkv = _matmul(h16, sb_w_qkv, slot, out_dtype=BF16)
                y = _sb_attention(qkv, heads=SB_HEADS)
                w_out = sb_w_out
            elif kind == 1:
                n_main = ml_w_in.shape[-1] - 2 * ML_HEADS
                qkvo = _matmul(h16, ml_w_in, slot, n=n_main, out_dtype=BF16)
                pad = V7X_LANES - 2 * ML_HEADS
                w_g = jnp.pad(ml_w_in[slot, :, n_main:], ((0, 0), (0, pad)))
                b_g = jnp.pad(ml_b_gates[slot].reshape(1, -1), ((0, 0), (0, pad)))
                gates = _ml_gates(h, w_g, b_g)
                y = _mlstm(qkvo, gates, ml_norm_w[slot], heads=ML_HEADS)
                w_out = ml_w_out
            else:
                bcu = _matmul(h16, sc_w_in, slot, out_dtype=BF16)
                y = _short_conv(bcu, sc_conv_w[slot])
                w_out = sc_w_out
            mix = _matmul(y, w_out, slot, out_dtype=BF16)
            h, h16 = _residual_ln(h, mix, ln_g[layer, 0], ln_b[layer, 0])
            mid = _matmul(h16, mlp_w1, layer, out_dtype=BF16, act="relu2")
            ff = _matmul_ksplit(mid, mlp_w2, layer, out_dtype=BF16)
            h, h16 = _residual_ln(h, ff, ln_g[layer, 1], ln_b[layer, 1])
        outs.append(h)
    return outs[0][None] if bsz == 1 else jnp.stack(outs, axis=0)
```
